```python
import math
import jax, jax.numpy as jnp
from jax import lax
import numpy as np


D_MODEL = 2048
BATCH = 1
SEQ = 8192
DEPTH = 1
DEC_BATCH = 8
DEC_SEQ = 2048
PAST_LEN = 128

D_MIX = D_MODEL
D_RWKV = D_MIX // 2
D_DIFF = D_MIX - D_RWKV
RWKV_HEAD = 64
N_RWKV_HEADS = D_RWKV // RWKV_HEAD
DIFF_VDIM = 128
N_DIFF_HEADS = D_DIFF // DIFF_VDIM
DIFF_QK = DIFF_VDIM // 2
LORA_W = 64
LORA_A = 64
ROPE_DIMS = DIFF_QK // 4
ROPE_THETA = 500000.0
Q_BLOCK = 128
RMS_EPS = 1e-6
GN_EPS = 64e-5
DECAY_SCALE = 0.606531

SHIFT_COLS = 3 * D_RWKV + LORA_W + LORA_A
D_IN = SHIFT_COLS + D_RWKV + 4 * D_DIFF

kernel_name = 'hymba_rwkv7_diffattn_bidir_encoder'


def rms_norm(x, g, eps=RMS_EPS):
    xf = x.astype(jnp.float32)
    y = xf * lax.rsqrt(jnp.mean(xf * xf, axis=-1, keepdims=True) + eps)
    return y * g.astype(jnp.float32)


def centred_shift(z, mu):
    prev = jnp.pad(z[:, :-1], ((0, 0), (1, 0), (0, 0)))
    nxt = jnp.pad(z[:, 1:], ((0, 0), (0, 1), (0, 0)))
    return z + mu * (0.5 * (prev + nxt) - z)


def rope_partial(x):
    T = x.shape[1]
    half = ROPE_DIMS // 2
    inv = ROPE_THETA ** (-jnp.arange(half, dtype=jnp.float32) * 2.0 / ROPE_DIMS)
    ang = jnp.arange(T, dtype=jnp.float32)[:, None] * inv[None, :]
    cos = jnp.cos(ang)[None, :, None, None, :]
    sin = jnp.sin(ang)[None, :, None, None, :]
    x1 = x[..., :half]
    x2 = x[..., half:ROPE_DIMS]
    return jnp.concatenate([x1 * cos - x2 * sin, x2 * cos + x1 * sin, x[..., ROPE_DIMS:]], axis=-1)


def rwkv7_scan(r, w, k, v, kk, b, reverse):
    B, T, H, N = r.shape
    seq = tuple(jnp.moveaxis(t, 1, 0) for t in (r, w, k, v, kk, b))

    def step(S, inp):
        r_t, w_t, k_t, v_t, kk_t, b_t = inp
        sa = jnp.einsum('bhvk,bhk->bhv', S, -kk_t)
        S = S * w_t[:, :, None, :] + sa[..., None] * b_t[:, :, None, :] + v_t[..., None] * k_t[:, :, None, :]
        return S, jnp.einsum('bhvk,bhk->bhv', S, r_t)

    S0 = jnp.zeros((B, H, N, N), jnp.float32)
    _, o = lax.scan(step, S0, seq, reverse=reverse)
    return jnp.moveaxis(o, 0, 1)


def rwkv7_mixer(zs, g, w0, w_up, a0, a_up, k_k, k_a, r_k, gn_gain, gn_bias):
    B, T, _ = zs.shape
    H, N = N_RWKV_HEADS, RWKV_HEAD
    hs = lambda t: t.reshape(B, T, H, N)
    r = zs[..., :D_RWKV]
    k = zs[..., D_RWKV:2 * D_RWKV]
    v = zs[..., 2 * D_RWKV:3 * D_RWKV]
    wd = zs[..., 3 * D_RWKV:3 * D_RWKV + LORA_W]
    ad = zs[..., 3 * D_RWKV + LORA_W:]
    kk = hs(k * k_k)
    kk = kk * lax.rsqrt(jnp.maximum(jnp.sum(kk * kk, axis=-1, keepdims=True), 1e-12))
    o = jnp.zeros((B, T, H, N), jnp.float32)
    for d, rev in enumerate((False, True)):
        w = jnp.exp(-DECAY_SCALE * jax.nn.sigmoid(w0[d] + jnp.tanh(wd) @ w_up[d]))
        a = jax.nn.sigmoid(a0[d] + ad @ a_up[d])
        k_d = k * (1.0 + (a - 1.0) * k_a)
        o = o + rwkv7_scan(hs(r), hs(w), hs(k_d), hs(v), kk, hs(a) * kk, rev)
    mean = jnp.mean(o, axis=-1, keepdims=True)
    var = jnp.mean(jnp.square(o - mean), axis=-1, keepdims=True)
    on = ((o - mean) * lax.rsqrt(var + GN_EPS)).reshape(B, T, D_RWKV) * gn_gain + gn_bias
    bonus = jnp.sum(hs(r) * hs(k) * r_k, axis=-1, keepdims=True) * hs(v)
    return (on + bonus.reshape(B, T, D_RWKV)) * jax.nn.silu(g)


def diff_attn_mixer(q, k, v, g, q_gain, k_gain, lambda_qk, subln_gain, lambda_init):
    B, T, _ = q.shape
    H = N_DIFF_HEADS
    qh = rope_partial(rms_norm(q.reshape(B, T, H, 2, DIFF_QK), q_gain))
    kh = rope_partial(rms_norm(k.reshape(B, T, H, 2, DIFF_QK), k_gain))
    vh = v.reshape(B, T, H, DIFF_VDIM)
    scale = DIFF_QK ** -0.5
    lam = (jnp.exp(jnp.sum(lambda_qk[0] * lambda_qk[1])) - jnp.exp(jnp.sum(lambda_qk[2] * lambda_qk[3]))
           + lambda_init)
    nb = T // Q_BLOCK
    qb = jnp.swapaxes(qh.reshape(B, nb, Q_BLOCK, H, 2, DIFF_QK), 0, 1)

    def block(qi):
        s = jnp.einsum('bqhcd,bkhcd->bhcqk', qi, kh) * scale
        p = jax.nn.softmax(s, axis=-1)
        attn = p[:, :, 0] - lam * p[:, :, 1]
        return jnp.einsum('bhqk,bkhd->bqhd', attn, vh)

    o = jnp.swapaxes(lax.map(block, qb), 0, 1).reshape(B, T, H, DIFF_VDIM)
    o = rms_norm(o, subln_gain) * (1.0 - lambda_init)
    return o.reshape(B, T, D_DIFF) * jax.nn.silu(g)


def encoder_layer(x, lambda_init, norm_gain, w_in, mu_shift, w0, w_up, a0, a_up, k_k, k_a, r_k,
                  gn_gain, gn_bias, q_norm_gain, k_norm_gain, lambda_qk, subln_gain, w_out):
    f32 = jnp.float32
    h = rms_norm(x, norm_gain)
    proj = h @ w_in.astype(f32)
    zs = centred_shift(proj[..., :SHIFT_COLS], mu_shift.astype(f32))
    g_r = proj[..., SHIFT_COLS:SHIFT_COLS + D_RWKV]
    off = SHIFT_COLS + D_RWKV
    q_d = proj[..., off:off + D_DIFF]
    k_d = proj[..., off + D_DIFF:off + 2 * D_DIFF]
    v_d = proj[..., off + 2 * D_DIFF:off + 3 * D_DIFF]
    g_d = proj[..., off + 3 * D_DIFF:off + 4 * D_DIFF]
    y_r = rwkv7_mixer(zs, g_r, w0.astype(f32), w_up.astype(f32), a0.astype(f32), a_up.astype(f32),
                      k_k.astype(f32), k_a.astype(f32), r_k.astype(f32), gn_gain.astype(f32), gn_bias.astype(f32))
    y_d = diff_attn_mixer(q_d, k_d, v_d, g_d, q_norm_gain, k_norm_gain, lambda_qk.astype(f32),
                          subln_gain, lambda_init)
    out = jnp.concatenate([y_r, y_d], axis=-1) @ w_out.astype(f32)
    return (x.astype(f32) + out).astype(x.dtype)


def setup_inputs(seed: int = 0) -> dict:
    key = jax.random.key(seed)
    ks = jax.random.split(key, 20)
    nrm = lambda k, shape, s: s * jax.random.normal(k, shape, jnp.float32)
    return {
        'x_prompt': nrm(ks[0], (BATCH, SEQ, D_MODEL), 1.0),
        'x_sample': nrm(ks[1], (DEC_BATCH, DEC_SEQ, D_MODEL), 1.0),
        'norm_gain': 1.0 + nrm(ks[2], (DEPTH, D_MODEL), 0.02),
        'w_in': nrm(ks[3], (DEPTH, D_MODEL, D_IN), D_MODEL ** -0.5),
        'mu_shift': 0.5 + nrm(ks[4], (DEPTH, SHIFT_COLS), 0.1),
        'w0': nrm(ks[5], (DEPTH, 2, D_RWKV), 1.0),
        'w_up': nrm(ks[6], (DEPTH, 2, LORA_W, D_RWKV), 0.1 * LORA_W ** -0.5),
        'a0': nrm(ks[7], (DEPTH, 2, D_RWKV), 0.5),
        'a_up': nrm(ks[8], (DEPTH, 2, LORA_A, D_RWKV), 0.1 * LORA_A ** -0.5),
        'k_k': 0.85 + nrm(ks[9], (DEPTH, D_RWKV), 0.05),
        'k_a': 1.0 + nrm(ks[10], (DEPTH, D_RWKV), 0.05),
        'r_k': nrm(ks[11], (DEPTH, N_RWKV_HEADS, RWKV_HEAD), 0.1),
        'gn_gain': 1.0 + nrm(ks[12], (DEPTH, D_RWKV), 0.02),
        'gn_bias': nrm(ks[13], (DEPTH, D_RWKV), 0.02),
        'q_norm_gain': 1.0 + nrm(ks[14], (DEPTH, DIFF_QK), 0.02),
        'k_norm_gain': 1.0 + nrm(ks[15], (DEPTH, DIFF_QK), 0.02),
        'lambda_qk': nrm(ks[16], (DEPTH, 4, DIFF_QK), 0.1),
        'subln_gain': 1.0 + nrm(ks[17], (DEPTH, DIFF_VDIM), 0.02),
        'w_out': nrm(ks[18], (DEPTH, D_MIX, D_MODEL), D_MIX ** -0.5),
    }


def reference(x_prompt, x_sample, norm_gain, w_in, mu_shift, w0, w_up, a0, a_up, k_k, k_a, r_k,
              gn_gain, gn_bias, q_norm_gain, k_norm_gain, lambda_qk, subln_gain, w_out):
    y_prompt = x_prompt
    y_sample = x_sample
    for l in range(DEPTH):
        lambda_init = 0.8 - 0.6 * math.exp(-0.3 * l)
        params = (norm_gain[l], w_in[l], mu_shift[l], w0[l], w_up[l], a0[l], a_up[l], k_k[l], k_a[l],
                  r_k[l], gn_gain[l], gn_bias[l], q_norm_gain[l], k_norm_gain[l], lambda_qk[l],
                  subln_gain[l], w_out[l])
        y_prompt = encoder_layer(y_prompt, lambda_init, *params)
        y_sample = encoder_layer(y_sample, lambda_init, *params)
    return (y_prompt, y_sample)
```

```python
import functools
import math

import jax
import jax.numpy as jnp
from jax import lax
from jax.experimental import pallas as pl
from jax.experimental.pallas import tpu as pltpu

F32 = jnp.float32
BF16 = jnp.bfloat16

D_MODEL = 2048
D_RWKV = 1024
D_DIFF = 1024
RWKV_HEAD = 64
DIFF_VDIM = 128
DIFF_QK = 64
N_DIFF_HEADS = D_DIFF // DIFF_VDIM
LORA = 64
ROPE_DIMS = DIFF_QK // 4
ROPE_HALF = ROPE_DIMS // 2
ROPE_THETA = 500000.0
RMS_EPS = 1e-6
GN_EPS = 64e-5
DECAY_SCALE = 0.606531
LAMBDA_INIT = 0.8 - 0.6 * math.exp(-0.3 * 0)

LANES = 128
HALO = 8
CHUNK = 64
PAIRS = D_RWKV // LANES
D_MAIN = 8 * 1024
VMEM_LIMIT = 56 * 1024 * 1024

COL_R, COL_K, COL_V, COL_GR, COL_Q, COL_KD, COL_VD, COL_GD = range(8)


def _dot(a, b):
    return jnp.dot(a, b, preferred_element_type=F32)


def _dot_nt(a, b):
    return lax.dot_general(a, b, (((1,), (1,)), ((), ())), preferred_element_type=F32)


def _dot_tn(a, b):
    return lax.dot_general(a, b, (((0,), (0,)), ((), ())), preferred_element_type=F32)


def _split_dot(x, m_bf16):
    hi = x.astype(BF16)
    lo = (x - hi.astype(F32)).astype(BF16)
    return _dot(hi, m_bf16) + _dot(lo, m_bf16)


def _sigmoid(x):
    return 1.0 / (1.0 + jnp.exp(-x))


def _seg_ones(width):
    r = lax.broadcasted_iota(jnp.int32, (LANES, LANES), 0) // width
    c = lax.broadcasted_iota(jnp.int32, (LANES, LANES), 1) // width
    return (r == c).astype(BF16)


def _in_proj_kernel(x_ref, g_ref, w_ref, wl_ref, o_ref, ol_ref, h_scr):
    @pl.when(pl.program_id(1) == 0)
    def _():
        x = x_ref[...]
        ms = jnp.mean(x * x, axis=-1, keepdims=True)
        h = (x * lax.rsqrt(ms + RMS_EPS) * g_ref[...]).astype(BF16)
        h_scr[...] = h
        ol_ref[...] = _dot(h, wl_ref[...])

    o_ref[...] = _dot(h_scr[...], w_ref[...])


def _in_proj(x2, gain, w_main, w_lora, tm=512, tn=1024):
    n = x2.shape[0]
    return pl.pallas_call(
        _in_proj_kernel,
        grid=(n // tm, D_MAIN // tn),
        in_specs=[
            pl.BlockSpec((tm, D_MODEL), lambda i, j: (i, 0)),
            pl.BlockSpec((1, D_MODEL), lambda i, j: (0, 0)),
            pl.BlockSpec((D_MODEL, tn), lambda i, j: (0, j)),
            pl.BlockSpec((D_MODEL, 2 * LORA), lambda i, j: (0, 0)),
        ],
        out_specs=[
            pl.BlockSpec((tm, tn), lambda i, j: (i, j)),
            pl.BlockSpec((tm, 2 * LORA), lambda i, j: (i, 0)),
        ],
        out_shape=[
            jax.ShapeDtypeStruct((n, D_MAIN), F32),
            jax.ShapeDtypeStruct((n, 2 * LORA), F32),
        ],
        scratch_shapes=[pltpu.VMEM((tm, D_MODEL), BF16)],
        compiler_params=pltpu.CompilerParams(
            dimension_semantics=("arbitrary", "arbitrary"), vmem_limit_bytes=VMEM_LIMIT),
        name="in_proj",
    )(x2, gain, w_main, w_lora)


def _shifted(x, prev_row, next_row, mu, row):
    tb = x.shape[0]
    prev = jnp.where(row == 0, prev_row, pltpu.roll(x, 1, 0))
    nxt = jnp.where(row == tb - 1, next_row, pltpu.roll(x, tb - 1, 0))
    return x + mu * (0.5 * (prev + nxt) - x)


def _block_diag(x, bd_mask):
    xb = x.astype(BF16)
    return jnp.where(bd_mask, jnp.concatenate([xb, xb], axis=0), jnp.zeros((), BF16))


def _rwkv_kernel(rev, final, tb, *refs):
    (r_ref, rp_ref, rn_ref, k_ref, kp_ref, kn_ref, v_ref, vp_ref, vn_ref,
     l_ref, lp_ref, ln_ref, mur_ref, muk_ref, muv_ref, mul_ref,
     w0_ref, wup_ref, a0_ref, aup_ref, kk_ref, ka_ref) = refs[:22]
    if final:
        (rk_ref, gng_ref, gnb_ref, g_ref, of_ref, out_ref,
         rs, kd, vs, kks, bs, lw, ob, ks, state) = refs[22:]
    else:
        out_ref, rs, kd, vs, kks, bs, lw, state = refs[22:]
        ob = out_ref.at[0]

    i = pl.program_id(1)
    nblk = pl.num_programs(1)
    blk = (nblk - 1 - i) if rev else i

    @pl.when(i == 0)
    def _():
        state[...] = jnp.zeros_like(state)

    row = lax.broadcasted_iota(jnp.int32, (tb, 1), 0)
    has_prev = (blk > 0).astype(F32)
    has_next = (blk < nblk - 1).astype(F32)

    def shift(ref, p_ref, n_ref, mu_ref):
        return _shifted(ref[0], p_ref[0, HALO - 1:HALO, :] * has_prev, n_ref[0, 0:1, :] * has_next,
                        mu_ref[...], row)

    r_s = shift(r_ref, rp_ref, rn_ref, mur_ref)
    k_s = shift(k_ref, kp_ref, kn_ref, muk_ref)
    v_s = shift(v_ref, vp_ref, vn_ref, muv_ref)
    z = shift(l_ref, lp_ref, ln_ref, mul_ref)
    lane = lax.broadcasted_iota(jnp.int32, (tb, 2 * LORA), 1)
    zt = jnp.where(lane < LORA, jnp.tanh(z), z).astype(BF16)
    logw = -DECAY_SCALE * _sigmoid(w0_ref[...] + _dot(zt, wup_ref[...]))
    asig = _sigmoid(a0_ref[...] + _dot(zt, aup_ref[...]))

    ones64 = _seg_ones(RWKV_HEAD)
    kk = k_s * kk_ref[...]
    sq = kk * kk
    ss = jnp.concatenate(
        [_split_dot(sq[:, p * LANES:(p + 1) * LANES], ones64) for p in range(PAIRS)], axis=1)
    kk = kk * lax.rsqrt(jnp.maximum(ss, 1e-12))

    rs[...] = r_s
    vs[...] = v_s
    kd[...] = k_s * (1.0 + (asig - 1.0) * ka_ref[...])
    kks[...] = kk
    bs[...] = asig * kk
    lw[...] = logw
    if final:
        ks[...] = k_s

    t_i = lax.broadcasted_iota(jnp.int32, (CHUNK, LANES), 0)
    j_i = lax.broadcasted_iota(jnp.int32, (CHUNK, LANES), 1) % CHUNK
    if rev:
        strict, incl = j_i > t_i, j_i >= t_i
    else:
        strict, incl = j_i < t_i, j_i <= t_i
    eye_pair = (j_i == t_i).astype(F32)
    tri = incl[:, :CHUNK].astype(BF16)
    rr = lax.broadcasted_iota(jnp.int32, (LANES, LANES), 0) // CHUNK
    cc = lax.broadcasted_iota(jnp.int32, (LANES, LANES), 1) // CHUNK
    bd_mask = rr == cc
    bd_mask2 = jnp.concatenate([bd_mask, bd_mask], axis=1)
    nchunk = tb // CHUNK

    def chunk_body(ci, carry):
        c = (nchunk - 1 - ci) if rev else ci
        rows = pl.ds(pl.multiple_of(c * CHUNK, CHUNK), CHUNK)
        for p in range(PAIRS):
            ln = slice(p * LANES, (p + 1) * LANES)
            lwc = lw[rows, ln]
            lw_hi = lwc.astype(BF16)
            lw_lo = (lwc - lw_hi.astype(F32)).astype(BF16)
            cum = _dot(tri, lw_hi) + _dot(tri, lw_lo)
            tot = jnp.sum(lwc, axis=0, keepdims=True)
            g_in = jnp.exp(cum)
            g_ex = jnp.exp(cum - lwc)
            g_inv = jnp.exp(-cum)
            g_rem = jnp.exp(tot - cum)
            kkc, bc, kdc, rc, vc = kks[rows, ln], bs[rows, ln], kd[rows, ln], rs[rows, ln], vs[rows, ln]
            a_t = -kkc * g_ex
            r_t = rc * g_in
            b_t = (bc * g_inv).astype(BF16)
            k_t = (kdc * g_inv).astype(BF16)
            b_h = (bc * g_rem).astype(BF16)
            k_h = (kdc * g_rem).astype(BF16)

            lhs = jnp.concatenate([a_t.astype(BF16), r_t.astype(BF16)], axis=0)
            zero = jnp.zeros((), BF16)
            rhs = jnp.concatenate([jnp.where(bd_mask, jnp.concatenate([b_t, b_t], axis=0), zero),
                                   jnp.where(bd_mask, jnp.concatenate([k_t, k_t], axis=0), zero)],
                                  axis=0)
            q = _dot_nt(lhs, rhs)
            l_ab = jnp.where(strict, q[:CHUNK, :LANES], 0.0)
            l_ak = jnp.where(strict, q[:CHUNK, LANES:], 0.0)
            m_rb = jnp.where(incl, q[CHUNK:, :LANES], 0.0)
            m_rk = jnp.where(incl, q[CHUNK:, LANES:], 0.0)

            lk = l_ab
            t_inv = eye_pair + l_ab
            lk = _dot(lk.astype(BF16), _block_diag(lk, bd_mask))
            for _ in range(4):
                res = _dot(jnp.concatenate([lk.astype(BF16), t_inv.astype(BF16)], axis=0),
                           _block_diag(lk, bd_mask))
                lk = res[:CHUNK]
                t_inv = t_inv + res[CHUNK:]
            t_inv = t_inv + _dot(t_inv.astype(BF16), _block_diag(lk, bd_mask))

            bd_v = _block_diag(vc, bd_mask)
            lak_v = _dot(l_ak.astype(BF16), bd_v)
            rhs2 = jnp.concatenate([_block_diag(a_t, bd_mask), _block_diag(lak_v, bd_mask)], axis=1)
            aw = _dot(t_inv.astype(BF16), rhs2)
            a_h, w2 = aw[:, :LANES], aw[:, LANES:]

            s = state[p]
            uo = _dot_nt(jnp.concatenate([a_h.astype(BF16), r_t.astype(BF16)], axis=0), s.astype(BF16))
            u = uo[:CHUNK] + w2
            o = uo[CHUNK:] + _dot(
                jnp.concatenate([m_rb.astype(BF16), m_rk.astype(BF16)], axis=1),
                jnp.concatenate([_block_diag(u, bd_mask), bd_v], axis=0))
            upd = _dot_tn(jnp.concatenate([u.astype(BF16), vc.astype(BF16)], axis=0),
                          jnp.concatenate([b_h, k_h], axis=0))
            state[p] = s * jnp.exp(tot) + jnp.where(bd_mask, upd, 0.0)
            ob[rows, ln] = o
        return carry

    lax.fori_loop(0, nchunk, chunk_body, 0)

    if final:
        o_all = ob[...] + of_ref[0]
        inv_n = 1.0 / RWKV_HEAD

        def seg(x):
            return jnp.concatenate(
                [_split_dot(x[:, p * LANES:(p + 1) * LANES], ones64) for p in range(PAIRS)], axis=1)

        mean = seg(o_all) * inv_n
        cen = o_all - mean
        var = seg(cen * cen) * inv_n
        on = cen * lax.rsqrt(var + GN_EPS) * gng_ref[...] + gnb_ref[...]
        bonus = seg(rs[...] * ks[...] * rk_ref[...]) * vs[...]
        g = g_ref[0]
        out_ref[0] = ((on + bonus) * (g * _sigmoid(g))).astype(out_ref.dtype)


def _rwkv_call(rev, final, proj, lora, params, extra, tb=256):
    b, t, _ = proj.shape
    nblk = t // tb
    hb = tb // HALO
    nhalo = t // HALO

    def blk(i):
        return (nblk - 1 - i) if rev else i

    def main_spec(col, width=1024):
        return pl.BlockSpec((1, tb, width), lambda bi, i: (bi, blk(i), col))

    def prev_spec(col, width=1024):
        return pl.BlockSpec((1, HALO, width), lambda bi, i: (bi, jnp.maximum(blk(i) * hb - 1, 0), col))

    def next_spec(col, width=1024):
        return pl.BlockSpec((1, HALO, width),
                            lambda bi, i: (bi, jnp.minimum((blk(i) + 1) * hb, nhalo - 1), col))

    def full_spec(a):
        return pl.BlockSpec(a.shape, lambda bi, i: (0,) * a.ndim)

    in_specs, args = [], []
    for col in (COL_R, COL_K, COL_V):
        in_specs += [main_spec(col), prev_spec(col), next_spec(col)]
        args += [proj, proj, proj]
    in_specs += [main_spec(0, 2 * LORA), prev_spec(0, 2 * LORA), next_spec(0, 2 * LORA)]
    args += [lora, lora, lora]
    for a in params:
        in_specs.append(full_spec(a))
        args.append(a)
    scratch = [pltpu.VMEM((tb, D_RWKV), F32) for _ in range(6)]
    if final:
        rk, gng, gnb, o_f = extra
        for a in (rk, gng, gnb):
            in_specs.append(full_spec(a))
            args.append(a)
        in_specs += [main_spec(COL_GR), pl.BlockSpec((1, tb, D_RWKV), lambda bi, i: (bi, blk(i), 0))]
        args += [proj, o_f]
        scratch += [pltpu.VMEM((tb, D_RWKV), F32), pltpu.VMEM((tb, D_RWKV), F32)]
        out_dtype = BF16
    else:
        out_dtype = F32
    scratch.append(pltpu.VMEM((PAIRS, LANES, LANES), F32))
    return pl.pallas_call(
        functools.partial(_rwkv_kernel, rev, final, tb),
        grid=(b, nblk),
        in_specs=in_specs,
        out_specs=pl.BlockSpec((1, tb, D_RWKV), lambda bi, i: (bi, blk(i), 0)),
        out_shape=jax.ShapeDtypeStruct((b, t, D_RWKV), out_dtype),
        scratch_shapes=scratch,
        compiler_params=pltpu.CompilerParams(
            dimension_semantics=("arbitrary", "arbitrary"), vmem_limit_bytes=VMEM_LIMIT),
        name="rwkv_bwd" if rev else "rwkv_fwd",
    )(*args)


def _attn_prep_kernel(q_ref, k_ref, v_ref, qg_ref, kg_ref, cos_ref, s1_ref, s2_ref,
                      qo_ref, kt_ref, vo_ref):
    ones64 = _seg_ones(DIFF_QK)
    cos, s1, s2 = cos_ref[...], s1_ref[...], s2_ref[...]

    def norm_rope(x, gain):
        ms = _split_dot(x * x, ones64) * (1.0 / DIFF_QK)
        y = x * lax.rsqrt(ms + RMS_EPS) * gain
        return y * cos + pltpu.roll(y, ROPE_HALF, 1) * s1 + pltpu.roll(y, LANES - ROPE_HALF, 1) * s2

    for h in range(N_DIFF_HEADS):
        ln = slice(h * LANES, (h + 1) * LANES)
        qh = norm_rope(q_ref[0, :, ln], qg_ref[...]) * (DIFF_QK ** -0.5)
        kh = norm_rope(k_ref[0, :, ln], kg_ref[...])
        qo_ref[0, :, ln] = qh.astype(BF16)
        kt_ref[0, 0, ln, :] = kh.T.astype(BF16)
    vo_ref[0] = v_ref[0].astype(BF16)


def _attn_prep(proj, q_gain, k_gain, cos, s1, s2, tk):
    b, t, _ = proj.shape
    nk = t // tk
    tab = pl.BlockSpec((tk, LANES), lambda bi, i: (i, 0))
    gain = pl.BlockSpec((1, LANES), lambda bi, i: (0, 0))
    return pl.pallas_call(
        _attn_prep_kernel,
        grid=(b, nk),
        in_specs=[
            pl.BlockSpec((1, tk, D_DIFF), lambda bi, i: (bi, i, COL_Q)),
            pl.BlockSpec((1, tk, D_DIFF), lambda bi, i: (bi, i, COL_KD)),
            pl.BlockSpec((1, tk, D_DIFF), lambda bi, i: (bi, i, COL_VD)),
            gain, gain, tab, tab, tab,
        ],
        out_specs=[
            pl.BlockSpec((1, tk, D_DIFF), lambda bi, i: (bi, i, 0)),
            pl.BlockSpec((1, 1, D_DIFF, tk), lambda bi, i: (bi, i, 0, 0)),
            pl.BlockSpec((1, tk, D_DIFF), lambda bi, i: (bi, i, 0)),
        ],
        out_shape=[
            jax.ShapeDtypeStruct((b, t, D_DIFF), BF16),
            jax.ShapeDtypeStruct((b, nk, D_DIFF, tk), BF16),
            jax.ShapeDtypeStruct((b, t, D_DIFF), BF16),
        ],
        compiler_params=pltpu.CompilerParams(
            dimension_semantics=("arbitrary", "arbitrary"), vmem_limit_bytes=VMEM_LIMIT),
        name="attn_prep",
    )(proj, proj, proj, q_gain, k_gain, cos, s1, s2)


def _diff_attn_kernel(nk, q_ref, kt_ref, v_ref, g_ref, lam_ref, sub_ref, o_ref):
    q = q_ref[0]
    tq = q.shape[0]
    tk = kt_ref.shape[3]
    lane = lax.broadcasted_iota(jnp.int32, (tq, LANES), 1)
    zero = jnp.zeros((), BF16)
    qq = jnp.concatenate([jnp.where(lane < DIFF_QK, q, zero), jnp.where(lane >= DIFF_QK, q, zero)],
                         axis=0)

    def body(kb, carry):
        m, l, acc = carry
        s = _dot(qq, kt_ref[0, kb])
        m_new = jnp.maximum(m, jnp.max(s, axis=-1, keepdims=True))
        p = jnp.exp(s - m_new)
        alpha = jnp.exp(m - m_new)
        l = alpha * l + jnp.sum(p, axis=-1, keepdims=True)
        vb = v_ref[0, pl.ds(pl.multiple_of(kb * tk, tk), tk), :]
        acc = alpha * acc + _dot(p.astype(BF16), vb)
        return m_new, l, acc

    m0 = jnp.full((2 * tq, 1), -jnp.inf, F32)
    l0 = jnp.zeros((2 * tq, 1), F32)
    a0 = jnp.zeros((2 * tq, LANES), F32)
    _, l, acc = lax.fori_loop(0, nk, body, (m0, l0, a0))
    on = acc / l
    lq = lam_ref[...]
    lam = (jnp.exp(jnp.sum(lq[0:1] * lq[1:2], axis=-1, keepdims=True))
           - jnp.exp(jnp.sum(lq[2:3] * lq[3:4], axis=-1, keepdims=True)) + LAMBDA_INIT)
    o = on[:tq] - lam * on[tq:]
    ms = jnp.mean(o * o, axis=-1, keepdims=True)
    y = o * lax.rsqrt(ms + RMS_EPS) * sub_ref[...] * (1.0 - LAMBDA_INIT)
    g = g_ref[0]
    o_ref[0] = (y * (g * _sigmoid(g))).astype(o_ref.dtype)


def _diff_attn(qs, kt, vb, proj, lambda_qk, subln, tq=256):
    b, t, _ = qs.shape
    nk, tk = kt.shape[1], kt.shape[3]
    gcol = COL_GD * (1024 // LANES)
    return pl.pallas_call(
        functools.partial(_diff_attn_kernel, nk),
        grid=(b, N_DIFF_HEADS, t // tq),
        in_specs=[
            pl.BlockSpec((1, tq, LANES), lambda bi, h, i: (bi, i, h)),
            pl.BlockSpec((1, nk, LANES, tk), lambda bi, h, i: (bi, 0, h, 0)),
            pl.BlockSpec((1, t, LANES), lambda bi, h, i: (bi, 0, h)),
            pl.BlockSpec((1, tq, LANES), lambda bi, h, i: (bi, i, gcol + h)),
            pl.BlockSpec((4, DIFF_QK), lambda bi, h, i: (0, 0)),
            pl.BlockSpec((1, LANES), lambda bi, h, i: (0, 0)),
        ],
        out_specs=pl.BlockSpec((1, tq, LANES), lambda bi, h, i: (bi, i, h)),
        out_shape=jax.ShapeDtypeStruct((b, t, D_DIFF), BF16),
        compiler_params=pltpu.CompilerParams(
            dimension_semantics=("arbitrary", "arbitrary", "arbitrary"), vmem_limit_bytes=VMEM_LIMIT),
        name="diff_attn",
    )(qs, kt, vb, proj, lambda_qk, subln)


def _out_proj_kernel(yr_ref, yd_ref, wr_ref, wd_ref, x_ref, o_ref):
    o_ref[...] = x_ref[...] + _dot(yr_ref[...], wr_ref[...]) + _dot(yd_ref[...], wd_ref[...])


def _out_proj(y_r, y_d, w_r, w_d, x2, tm=512, tn=1024):
    n = x2.shape[0]
    return pl.pallas_call(
        _out_proj_kernel,
        grid=(n // tm, D_MODEL // tn),
        in_specs=[
            pl.BlockSpec((tm, D_RWKV), lambda i, j: (i, 0)),
            pl.BlockSpec((tm, D_DIFF), lambda i, j: (i, 0)),
            pl.BlockSpec((D_RWKV, tn), lambda i, j: (0, j)),
            pl.BlockSpec((D_DIFF, tn), lambda i, j: (0, j)),
            pl.BlockSpec((tm, tn), lambda i, j: (i, j)),
        ],
        out_specs=pl.BlockSpec((tm, tn), lambda i, j: (i, j)),
        out_shape=jax.ShapeDtypeStruct((n, D_MODEL), F32),
        compiler_params=pltpu.CompilerParams(
            dimension_semantics=("arbitrary", "arbitrary"), vmem_limit_bytes=VMEM_LIMIT),
        name="out_proj",
    )(y_r, y_d, w_r, w_d, x2)


def _rope_tables(t):
    inv = ROPE_THETA ** (-jnp.arange(ROPE_HALF, dtype=F32) * 2.0 / ROPE_DIMS)
    ang = jnp.arange(t, dtype=F32)[:, None] * inv[None, :]
    cos, sin = jnp.cos(ang), jnp.sin(ang)
    ones = jnp.ones((t, DIFF_QK - ROPE_DIMS), F32)
    zeros_h = jnp.zeros((t, ROPE_HALF), F32)
    zeros_r = jnp.zeros((t, DIFF_QK - ROPE_DIMS), F32)
    c = jnp.concatenate([cos, cos, ones], axis=1)
    s1 = jnp.concatenate([zeros_h, sin, zeros_r], axis=1)
    s2 = jnp.concatenate([-sin, zeros_h, zeros_r], axis=1)
    return tuple(jnp.concatenate([a, a], axis=1) for a in (c, s1, s2))


def _prepare_weights(norm_gain, w_in, mu_shift, w0, w_up, a0, a_up, k_k, k_a, r_k, gn_gain, gn_bias,
                     q_norm_gain, k_norm_gain, subln_gain, w_out):
    d3 = 3 * D_RWKV
    shift_cols = d3 + 2 * LORA
    w_main = jnp.concatenate([w_in[:, :d3], w_in[:, shift_cols:]], axis=1).astype(BF16)
    w_lora = w_in[:, d3:shift_cols].astype(BF16)
    row = lambda a: a.reshape(1, -1).astype(F32)
    zpad = jnp.zeros((LORA, D_RWKV), F32)
    dirs = []
    for d in range(2):
        dirs.append((
            row(mu_shift[:D_RWKV]), row(mu_shift[D_RWKV:2 * D_RWKV]), row(mu_shift[2 * D_RWKV:d3]),
            row(mu_shift[d3:shift_cols]),
            row(w0[d]), jnp.concatenate([w_up[d], zpad], axis=0).astype(BF16),
            row(a0[d]), jnp.concatenate([zpad, a_up[d]], axis=0).astype(BF16),
            row(k_k), row(k_a)))
    final_extra = (row(r_k), row(gn_gain), row(gn_bias))
    attn = (row(jnp.tile(q_norm_gain, 2)), row(jnp.tile(k_norm_gain, 2)), row(subln_gain))
    w_o = w_out.astype(BF16)
    return row(norm_gain), w_main, w_lora, dirs, final_extra, attn, (w_o[:D_RWKV], w_o[D_RWKV:])


def _layer(x, weights, lambda_qk):
    gain, w_main, w_lora, dirs, final_extra, attn, (w_r, w_d) = weights
    b, t, _ = x.shape
    x2 = x.reshape(b * t, D_MODEL)
    proj2, lora2 = _in_proj(x2, gain, w_main, w_lora)
    proj = proj2.reshape(b, t, D_MAIN)
    lora = lora2.reshape(b, t, 2 * LORA)

    o_f = _rwkv_call(False, False, proj, lora, dirs[0], None)
    y_r = _rwkv_call(True, True, proj, lora, dirs[1], final_extra + (o_f,))

    tk = 512
    cos, s1, s2 = _rope_tables(t)
    q_gain, k_gain, subln = attn
    qs, kt, vb = _attn_prep(proj, q_gain, k_gain, cos, s1, s2, tk)
    y_d = _diff_attn(qs, kt, vb, proj, lambda_qk, subln)

    out = _out_proj(y_r.reshape(b * t, D_RWKV), y_d.reshape(b * t, D_DIFF), w_r, w_d, x2)
    return out.reshape(b, t, D_MODEL)


def kernel(x_prompt, x_sample, norm_gain, w_in, mu_shift, w0, w_up, a0, a_up, k_k, k_a, r_k, gn_gain,
           gn_bias, q_norm_gain, k_norm_gain, lambda_qk, subln_gain, w_out):
    weights = _prepare_weights(norm_gain[0], w_in[0], mu_shift[0], w0[0], w_up[0], a0[0], a_up[0],
                               k_k[0], k_a[0], r_k[0], gn_gain[0], gn_bias[0], q_norm_gain[0],
                               k_norm_gain[0], subln_gain[0], w_out[0])
    lam = lambda_qk[0].astype(F32)
    return (_layer(x_prompt, weights, lam), _layer(x_sample, weights, lam))
```

```python
import functools
import math

import jax
import jax.numpy as jnp
from jax import lax
from jax.experimental import pallas as pl
from jax.experimental.pallas import tpu as pltpu

F32 = jnp.float32
BF16 = jnp.bfloat16

D_MODEL = 2048
D_RWKV = 1024
D_DIFF = 1024
RWKV_HEAD = 64
DIFF_VDIM = 128
DIFF_QK = 64
N_DIFF_HEADS = D_DIFF // DIFF_VDIM
LORA = 64
ROPE_DIMS = DIFF_QK // 4
ROPE_HALF = ROPE_DIMS // 2
ROPE_THETA = 500000.0
RMS_EPS = 1e-6
GN_EPS = 64e-5
DECAY_SCALE = 0.606531
LAMBDA_INIT = 0.8 - 0.6 * math.exp(-0.3 * 0)
LOG2E = 1.4426950408889634

LANES = 128
HALO = 8
CHUNK = 64
PAIRS = D_RWKV // LANES
D_MAIN = 8 * 1024
VMEM_LIMIT = 56 * 1024 * 1024

COL_R, COL_K, COL_V, COL_GR, COL_Q, COL_KD, COL_VD, COL_GD = range(8)


def _dot(a, b):
    return jnp.dot(a, b, preferred_element_type=F32)


def _dot_nt(a, b):
    return lax.dot_general(a, b, (((1,), (1,)), ((), ())), preferred_element_type=F32)


def _dot_tn(a, b):
    return lax.dot_general(a, b, (((0,), (0,)), ((), ())), preferred_element_type=F32)


def _split(x):
    hi = x.astype(BF16)
    return hi, (x - hi.astype(F32)).astype(BF16)


def _split_dot(x, m_bf16):
    hi, lo = _split(x)
    return _dot(hi, m_bf16) + _dot(lo, m_bf16)


def _sigmoid(x):
    return 1.0 / (1.0 + jnp.exp(-x))


def _seg_ones(width):
    r = lax.broadcasted_iota(jnp.int32, (LANES, LANES), 0) // width
    c = lax.broadcasted_iota(jnp.int32, (LANES, LANES), 1) // width
    return (r == c).astype(BF16)


def _in_proj_kernel(x_ref, g_ref, w_ref, wl_ref, o_ref, ol_ref, h_scr):
    @pl.when(pl.program_id(1) == 0)
    def _():
        x = x_ref[...]
        ms = jnp.mean(x * x, axis=-1, keepdims=True)
        h = (x * lax.rsqrt(ms + RMS_EPS) * g_ref[...]).astype(BF16)
        h_scr[...] = h
        ol_ref[...] = _dot(h, wl_ref[...])

    o_ref[...] = _dot(h_scr[...], w_ref[...])


def _in_proj(x2, gain, w_main, w_lora, tm=512, tn=1024):
    n = x2.shape[0]
    return pl.pallas_call(
        _in_proj_kernel,
        grid=(n // tm, D_MAIN // tn),
        in_specs=[
            pl.BlockSpec((tm, D_MODEL), lambda i, j: (i, 0)),
            pl.BlockSpec((1, D_MODEL), lambda i, j: (0, 0)),
            pl.BlockSpec((D_MODEL, tn), lambda i, j: (0, j)),
            pl.BlockSpec((D_MODEL, 2 * LORA), lambda i, j: (0, 0)),
        ],
        out_specs=[
            pl.BlockSpec((tm, tn), lambda i, j: (i, j)),
            pl.BlockSpec((tm, 2 * LORA), lambda i, j: (i, 0)),
        ],
        out_shape=[
            jax.ShapeDtypeStruct((n, D_MAIN), F32),
            jax.ShapeDtypeStruct((n, 2 * LORA), F32),
        ],
        scratch_shapes=[pltpu.VMEM((tm, D_MODEL), BF16)],
        compiler_params=pltpu.CompilerParams(
            dimension_semantics=("arbitrary", "arbitrary"), vmem_limit_bytes=VMEM_LIMIT),
        name="in_proj",
    )(x2, gain, w_main, w_lora)


def _shifted(x, prev_row, next_row, mu, row):
    tb = x.shape[0]
    prev = jnp.where(row == 0, prev_row, pltpu.roll(x, 1, 0))
    nxt = jnp.where(row == tb - 1, next_row, pltpu.roll(x, tb - 1, 0))
    return x + mu * (0.5 * (prev + nxt) - x)


def _rwkv_kernel(rev, final, tb, *refs):
    (r_ref, rp_ref, rn_ref, k_ref, kp_ref, kn_ref, v_ref, vp_ref, vn_ref,
     l_ref, lp_ref, ln_ref, mur_ref, muk_ref, muv_ref, mul_ref,
     w0_ref, wup_ref, a0_ref, aup_ref, kk_ref, ka_ref) = refs[:22]
    if final:
        (rk_ref, gng_ref, gnb_ref, g_ref, of_ref, out_ref,
         a_s, r_s, bt_s, kt_s, bh_s, kh_s, v_s, dec_s, ob, bonus_s, state) = refs[22:]
    else:
        out_ref, a_s, r_s, bt_s, kt_s, bh_s, kh_s, v_s, dec_s, state = refs[22:]
        ob = out_ref.at[0]

    i = pl.program_id(1)
    nblk = pl.num_programs(1)
    blk = (nblk - 1 - i) if rev else i
    nchunk = tb // CHUNK

    @pl.when(i == 0)
    def _():
        state[...] = jnp.zeros_like(state)

    row = lax.broadcasted_iota(jnp.int32, (tb, 1), 0)
    has_prev = (blk > 0).astype(F32)
    has_next = (blk < nblk - 1).astype(F32)

    def shift(ref, p_ref, n_ref, mu_ref):
        return _shifted(ref[0], p_ref[0, HALO - 1:HALO, :] * has_prev, n_ref[0, 0:1, :] * has_next,
                        mu_ref[...], row)

    rr = shift(r_ref, rp_ref, rn_ref, mur_ref)
    kk_s = shift(k_ref, kp_ref, kn_ref, muk_ref)
    vv = shift(v_ref, vp_ref, vn_ref, muv_ref)
    z = shift(l_ref, lp_ref, ln_ref, mul_ref)
    lane = lax.broadcasted_iota(jnp.int32, (tb, 2 * LORA), 1)
    zt = jnp.where(lane < LORA, jnp.tanh(z), z).astype(BF16)
    logw = -DECAY_SCALE * _sigmoid(w0_ref[...] + _dot(zt, wup_ref[...]))
    asig = _sigmoid(a0_ref[...] + _dot(zt, aup_ref[...]))

    ones64 = _seg_ones(RWKV_HEAD)

    def seg(x):
        return jnp.concatenate(
            [_split_dot(x[:, p * LANES:(p + 1) * LANES], ones64) for p in range(PAIRS)], axis=1)

    kk = kk_s * kk_ref[...]
    kk = kk * lax.rsqrt(jnp.maximum(seg(kk * kk), 1e-12))
    kd = kk_s * (1.0 + (asig - 1.0) * ka_ref[...])
    bb = asig * kk
    if final:
        bonus_s[...] = seg(rr * kk_s * rk_ref[...]) * vv

    ti = lax.broadcasted_iota(jnp.int32, (tb, tb), 0)
    tj = lax.broadcasted_iota(jnp.int32, (tb, tb), 1)
    same = (ti // CHUNK) == (tj // CHUNK)
    order = (tj >= ti) if rev else (tj <= ti)
    tri_blk = (same & order).astype(BF16)
    ones_blk = same.astype(BF16)
    lw_hi, lw_lo = _split(logw)
    cum = _dot(tri_blk, lw_hi) + _dot(tri_blk, lw_lo)
    tot = _dot(ones_blk, lw_hi) + _dot(ones_blk, lw_lo)
    g_inv = jnp.exp(-cum)
    g_rem = jnp.exp(tot - cum)
    a_s[...] = (-kk * jnp.exp(cum - logw)).astype(BF16)
    r_s[...] = (rr * jnp.exp(cum)).astype(BF16)
    bt_s[...] = (bb * g_inv).astype(BF16)
    kt_s[...] = (kd * g_inv).astype(BF16)
    bh_s[...] = (bb * g_rem).astype(BF16)
    kh_s[...] = (kd * g_rem).astype(BF16)
    v_s[...] = vv.astype(BF16)
    dec = jnp.exp(tot)
    for c in range(nchunk):
        dec_s[c * HALO:(c + 1) * HALO, :] = dec[c * CHUNK:c * CHUNK + HALO, :]

    t_i = lax.broadcasted_iota(jnp.int32, (CHUNK, LANES), 0)
    j_i = lax.broadcasted_iota(jnp.int32, (CHUNK, LANES), 1) % CHUNK
    if rev:
        strict, incl = j_i > t_i, j_i >= t_i
    else:
        strict, incl = j_i < t_i, j_i <= t_i
    eye_pair = (j_i == t_i).astype(F32)
    bd_mask = (lax.broadcasted_iota(jnp.int32, (LANES, LANES), 0) // CHUNK
               == lax.broadcasted_iota(jnp.int32, (LANES, LANES), 1) // CHUNK)
    zero_b = jnp.zeros((), BF16)
    pairs = range(PAIRS)
    lns = [slice(p * LANES, (p + 1) * LANES) for p in pairs]

    def bd(xb):
        return jnp.where(bd_mask, jnp.concatenate([xb, xb], axis=0), zero_b)

    def stack(x, y):
        return jnp.concatenate([x, y], axis=0)

    def chunk_body(ci, carry):
        c = (nchunk - 1 - ci) if rev else ci
        rows = pl.ds(pl.multiple_of(c * CHUNK, CHUNK), CHUNK)
        drow = pl.ds(pl.multiple_of(c * HALO, HALO), 1)
        a_t = [a_s[rows, ln] for ln in lns]
        r_t = [r_s[rows, ln] for ln in lns]
        v_c = [v_s[rows, ln] for ln in lns]
        bd_v = [bd(v) for v in v_c]

        q = [_dot_nt(stack(a_t[p], r_t[p]), stack(bd(bt_s[rows, lns[p]]), bd(kt_s[rows, lns[p]])))
             for p in pairs]
        l_ab = [jnp.where(strict, x[:CHUNK, :LANES], 0.0) for x in q]
        l_ak = [jnp.where(strict, x[:CHUNK, LANES:], 0.0).astype(BF16) for x in q]
        m_r = [jnp.concatenate([jnp.where(incl, x[CHUNK:, :LANES], 0.0).astype(BF16),
                                jnp.where(incl, x[CHUNK:, LANES:], 0.0).astype(BF16)], axis=1) for x in q]
        lak_v = [_dot(l_ak[p], bd_v[p]) for p in pairs]

        t_inv = [eye_pair + x for x in l_ab]
        lk = [x.astype(BF16) for x in l_ab]
        lk = [_dot(x, bd(x)).astype(BF16) for x in lk]
        for _ in range(4):
            res = [_dot(stack(lk[p], t_inv[p].astype(BF16)), bd(lk[p])) for p in pairs]
            lk = [x[:CHUNK].astype(BF16) for x in res]
            t_inv = [t_inv[p] + res[p][CHUNK:] for p in pairs]
        t_inv = [(t_inv[p] + _dot(t_inv[p].astype(BF16), bd(lk[p]))).astype(BF16) for p in pairs]

        aw = [_dot(t_inv[p], jnp.concatenate([bd(a_t[p]), bd(lak_v[p].astype(BF16))], axis=1))
              for p in pairs]

        s = [state[p] for p in pairs]
        uo = [_dot_nt(stack(aw[p][:, :LANES].astype(BF16), r_t[p]), s[p].astype(BF16)) for p in pairs]
        u = [(uo[p][:CHUNK] + aw[p][:, LANES:]).astype(BF16) for p in pairs]
        o = [uo[p][CHUNK:] + _dot(m_r[p], stack(bd(u[p]), bd_v[p])) for p in pairs]
        upd = [_dot_tn(stack(u[p], v_c[p]), stack(bh_s[rows, lns[p]], kh_s[rows, lns[p]])) for p in pairs]
        for p in pairs:
            state[p] = s[p] * dec_s[drow, lns[p]] + jnp.where(bd_mask, upd[p], 0.0)
            ob[rows, lns[p]] = o[p]
        return carry

    lax.fori_loop(0, nchunk, chunk_body, 0)

    if final:
        o_all = ob[...] + of_ref[0]
        inv_n = 1.0 / RWKV_HEAD
        mean = seg(o_all) * inv_n
        cen = o_all - mean
        var = seg(cen * cen) * inv_n
        on = cen * lax.rsqrt(var + GN_EPS) * gng_ref[...] + gnb_ref[...]
        g = g_ref[0]
        out_ref[0] = ((on + bonus_s[...]) * (g * _sigmoid(g))).astype(out_ref.dtype)


def _rwkv_call(rev, final, proj, lora, params, extra, tb=256):
    b, t, _ = proj.shape
    nblk = t // tb
    hb = tb // HALO
    nhalo = t // HALO

    def blk(i):
        return (nblk - 1 - i) if rev else i

    def main_spec(col, width=1024):
        return pl.BlockSpec((1, tb, width), lambda bi, i: (bi, blk(i), col))

    def prev_spec(col, width=1024):
        return pl.BlockSpec((1, HALO, width), lambda bi, i: (bi, jnp.maximum(blk(i) * hb - 1, 0), col))

    def next_spec(col, width=1024):
        return pl.BlockSpec((1, HALO, width),
                            lambda bi, i: (bi, jnp.minimum((blk(i) + 1) * hb, nhalo - 1), col))

    def full_spec(a):
        return pl.BlockSpec(a.shape, lambda bi, i: (0,) * a.ndim)

    in_specs, args = [], []
    for col in (COL_R, COL_K, COL_V):
        in_specs += [main_spec(col), prev_spec(col), next_spec(col)]
        args += [proj, proj, proj]
    in_specs += [main_spec(0, 2 * LORA), prev_spec(0, 2 * LORA), next_spec(0, 2 * LORA)]
    args += [lora, lora, lora]
    for a in params:
        in_specs.append(full_spec(a))
        args.append(a)
    scratch = [pltpu.VMEM((tb, D_RWKV), BF16) for _ in range(7)]
    scratch.append(pltpu.VMEM((tb // CHUNK * HALO, D_RWKV), F32))
    if final:
        rk, gng, gnb, o_f = extra
        for a in (rk, gng, gnb):
            in_specs.append(full_spec(a))
            args.append(a)
        in_specs += [main_spec(COL_GR), pl.BlockSpec((1, tb, D_RWKV), lambda bi, i: (bi, blk(i), 0))]
        args += [proj, o_f]
        scratch += [pltpu.VMEM((tb, D_RWKV), F32), pltpu.VMEM((tb, D_RWKV), F32)]
        out_dtype = BF16
    else:
        out_dtype = F32
    scratch.append(pltpu.VMEM((PAIRS, LANES, LANES), F32))
    return pl.pallas_call(
        functools.partial(_rwkv_kernel, rev, final, tb),
        grid=(b, nblk),
        in_specs=in_specs,
        out_specs=pl.BlockSpec((1, tb, D_RWKV), lambda bi, i: (bi, blk(i), 0)),
        out_shape=jax.ShapeDtypeStruct((b, t, D_RWKV), out_dtype),
        scratch_shapes=scratch,
        compiler_params=pltpu.CompilerParams(
            dimension_semantics=("arbitrary", "arbitrary"), vmem_limit_bytes=VMEM_LIMIT),
        name="rwkv_bwd" if rev else "rwkv_fwd",
    )(*args)


def _attn_prep_kernel(q_ref, k_ref, v_ref, qg_ref, kg_ref, cos_ref, s1_ref, s2_ref,
                      qt_ref, ko_ref, vt_ref):
    ones64 = _seg_ones(DIFF_QK)
    cos, s1, s2 = cos_ref[...], s1_ref[...], s2_ref[...]

    def norm_rope(x, gain):
        ms = _split_dot(x * x, ones64) * (1.0 / DIFF_QK)
        y = x * lax.rsqrt(ms + RMS_EPS) * gain
        return y * cos + pltpu.roll(y, ROPE_HALF, 1) * s1 + pltpu.roll(y, LANES - ROPE_HALF, 1) * s2

    for h in range(N_DIFF_HEADS):
        ln = slice(h * LANES, (h + 1) * LANES)
        qh = norm_rope(q_ref[0, :, ln], qg_ref[...]) * (DIFF_QK ** -0.5 * LOG2E)
        qt_ref[0, ln, :] = qh.T.astype(BF16)
        ko_ref[0, :, ln] = norm_rope(k_ref[0, :, ln], kg_ref[...]).astype(BF16)
        vt_ref[0, 0, ln, :] = v_ref[0, :, ln].T.astype(BF16)


def _attn_prep(proj, q_gain, k_gain, cos, s1, s2, tk):
    b, t, _ = proj.shape
    nk = t // tk
    tab = pl.BlockSpec((tk, LANES), lambda bi, i: (i, 0))
    gain = pl.BlockSpec((1, LANES), lambda bi, i: (0, 0))
    return pl.pallas_call(
        _attn_prep_kernel,
        grid=(b, nk),
        in_specs=[
            pl.BlockSpec((1, tk, D_DIFF), lambda bi, i: (bi, i, COL_Q)),
            pl.BlockSpec((1, tk, D_DIFF), lambda bi, i: (bi, i, COL_KD)),
            pl.BlockSpec((1, tk, D_DIFF), lambda bi, i: (bi, i, COL_VD)),
            gain, gain, tab, tab, tab,
        ],
        out_specs=[
            pl.BlockSpec((1, D_DIFF, tk), lambda bi, i: (bi, 0, i)),
            pl.BlockSpec((1, tk, D_DIFF), lambda bi, i: (bi, i, 0)),
            pl.BlockSpec((1, 1, D_DIFF, tk), lambda bi, i: (bi, i, 0, 0)),
        ],
        out_shape=[
            jax.ShapeDtypeStruct((b, D_DIFF, t), BF16),
            jax.ShapeDtypeStruct((b, t, D_DIFF), BF16),
            jax.ShapeDtypeStruct((b, nk, D_DIFF, tk), BF16),
        ],
        compiler_params=pltpu.CompilerParams(
            dimension_semantics=("arbitrary", "arbitrary"), vmem_limit_bytes=VMEM_LIMIT),
        name="attn_prep",
    )(proj, proj, proj, q_gain, k_gain, cos, s1, s2)


def _diff_attn_kernel(nk, qt_ref, k_ref, vt_ref, g_ref, lam_ref, sub_ref, o_ref,
                      sa_ref, sb_ref, pa_ref, pb_ref):
    qt = qt_ref[0]
    tq = qt.shape[1]
    tk = vt_ref.shape[3]
    sub = lax.broadcasted_iota(jnp.int32, (LANES, tq), 0)
    zero = jnp.zeros((), BF16)
    qq = jnp.concatenate([jnp.where(sub < DIFF_QK, qt, zero), jnp.where(sub >= DIFF_QK, qt, zero)],
                         axis=1)

    def scores(kb, s_ref):
        kblk = k_ref[0, pl.ds(pl.multiple_of(kb * tk, tk), tk), :]
        s_ref[...] = _dot(kblk, qq)

    def softmax_step(s_ref, p_ref, m, l):
        ms, ls, als = [], [], []
        row_chunks = [slice(r * LANES, (r + 1) * LANES) for r in range(tk // LANES)]
        for j in range(2 * tq // LANES):
            ln = slice(j * LANES, (j + 1) * LANES)
            m_new = m[:, ln]
            for rc in row_chunks:
                m_new = jnp.maximum(m_new, jnp.max(s_ref[rc, ln], axis=0, keepdims=True))
            alpha = jnp.exp2(m[:, ln] - m_new)
            l_new = alpha * l[:, ln]
            for rc in row_chunks:
                p = jnp.exp2(s_ref[rc, ln] - m_new)
                p_ref[rc, ln] = p.astype(BF16)
                l_new = l_new + jnp.sum(p, axis=0, keepdims=True)
            ms.append(m_new)
            als.append(alpha)
            ls.append(l_new)
        cat = lambda xs: jnp.concatenate(xs, axis=1)
        return cat(als), cat(ms), cat(ls)

    m = jnp.full((1, 2 * tq), -jnp.inf, F32)
    l = jnp.zeros((1, 2 * tq), F32)
    acc = jnp.zeros((LANES, 2 * tq), F32)
    scores(0, sa_ref)
    al_a, m, l = softmax_step(sa_ref, pa_ref, m, l)
    scores(1, sb_ref)

    def stage(kb, acc, alpha, m, l, s_mine, p_mine, s_other, p_other, last):
        if not last:
            scores(kb + 2, s_mine)
        pv = _dot(vt_ref[0, kb], p_mine[...])
        alpha_other, m, l = softmax_step(s_other, p_other, m, l)
        return alpha * acc + pv, alpha_other, m, l

    def body(i, carry):
        m, l, acc, al_a = carry
        kb = 2 * i
        acc, al_b, m, l = stage(kb, acc, al_a, m, l, sa_ref, pa_ref, sb_ref, pb_ref, False)
        acc, al_a, m, l = stage(kb + 1, acc, al_b, m, l, sb_ref, pb_ref, sa_ref, pa_ref, False)
        return m, l, acc, al_a

    m, l, acc, al_a = lax.fori_loop(0, (nk - 2) // 2, body, (m, l, acc, al_a))
    acc, al_b, m, l = stage(nk - 2, acc, al_a, m, l, sa_ref, pa_ref, sb_ref, pb_ref, True)
    acc = al_b * acc + _dot(vt_ref[0, nk - 1], pb_ref[...])
    on = acc * (1.0 / l)
    lq = lam_ref[...]
    lam = (jnp.exp(jnp.sum(lq[0:1] * lq[1:2], axis=-1, keepdims=True))
           - jnp.exp(jnp.sum(lq[2:3] * lq[3:4], axis=-1, keepdims=True)) + LAMBDA_INIT)
    o = (on[:, :tq] - lam * on[:, tq:]).T
    ms = jnp.mean(o * o, axis=-1, keepdims=True)
    y = o * lax.rsqrt(ms + RMS_EPS) * sub_ref[...] * (1.0 - LAMBDA_INIT)
    g = g_ref[0]
    o_ref[0] = (y * (g * _sigmoid(g))).astype(o_ref.dtype)


def _diff_attn(qt, kn, vt, proj, lambda_qk, subln, tq=256):
    b, _, t = qt.shape
    nk, tk = vt.shape[1], vt.shape[3]
    assert nk >= 2 and nk % 2 == 0
    gcol =COL_GD * (1024 // LANES)
    return pl.pallas_call(
        functools.partial(_diff_attn_kernel, nk),
        grid=(b, N_DIFF_HEADS, t // tq),
        in_specs=[
            pl.BlockSpec((1, LANES, tq), lambda bi, h, i: (bi, h, i)),
            pl.BlockSpec((1, t, LANES), lambda bi, h, i: (bi, 0, h)),
            pl.BlockSpec((1, nk, LANES, tk), lambda bi, h, i: (bi, 0, h, 0)),
            pl.BlockSpec((1, tq, LANES), lambda bi, h, i: (bi, i, gcol + h)),
            pl.BlockSpec((4, DIFF_QK), lambda bi, h, i: (0, 0)),
            pl.BlockSpec((1, LANES), lambda bi, h, i: (0, 0)),
        ],
        out_specs=pl.BlockSpec((1, tq, LANES), lambda bi, h, i: (bi, i, h)),
        out_shape=jax.ShapeDtypeStruct((b, t, D_DIFF), BF16),
        scratch_shapes=[pltpu.VMEM((tk, 2 * tq), F32), pltpu.VMEM((tk, 2 * tq), F32),
                        pltpu.VMEM((tk, 2 * tq), BF16), pltpu.VMEM((tk, 2 * tq), BF16)],
        compiler_params=pltpu.CompilerParams(
            dimension_semantics=("arbitrary", "arbitrary", "arbitrary"), vmem_limit_bytes=VMEM_LIMIT),
        name="diff_attn",
    )(qt, kn, vt, proj, lambda_qk, subln)


def _out_proj_kernel(yr_ref, yd_ref, wr_ref, wd_ref, x_ref, o_ref):
    o_ref[...] = x_ref[...] + _dot(yr_ref[...], wr_ref[...]) + _dot(yd_ref[...], wd_ref[...])


def _out_proj(y_r, y_d, w_r, w_d, x2, tm=512, tn=1024):
    n = x2.shape[0]
    return pl.pallas_call(
        _out_proj_kernel,
        grid=(n // tm, D_MODEL // tn),
        in_specs=[
            pl.BlockSpec((tm, D_RWKV), lambda i, j: (i, 0)),
            pl.BlockSpec((tm, D_DIFF), lambda i, j: (i, 0)),
            pl.BlockSpec((D_RWKV, tn), lambda i, j: (0, j)),
            pl.BlockSpec((D_DIFF, tn), lambda i, j: (0, j)),
            pl.BlockSpec((tm, tn), lambda i, j: (i, j)),
        ],
        out_specs=pl.BlockSpec((tm, tn), lambda i, j: (i, j)),
        out_shape=jax.ShapeDtypeStruct((n, D_MODEL), F32),
        compiler_params=pltpu.CompilerParams(
            dimension_semantics=("arbitrary", "arbitrary"), vmem_limit_bytes=VMEM_LIMIT),
        name="out_proj",
    )(y_r, y_d, w_r, w_d, x2)


def _rope_tables(t):
    inv = ROPE_THETA ** (-jnp.arange(ROPE_HALF, dtype=F32) * 2.0 / ROPE_DIMS)
    ang = jnp.arange(t, dtype=F32)[:, None] * inv[None, :]
    cos, sin = jnp.cos(ang), jnp.sin(ang)
    ones = jnp.ones((t, DIFF_QK - ROPE_DIMS), F32)
    zeros_h = jnp.zeros((t, ROPE_HALF), F32)
    zeros_r = jnp.zeros((t, DIFF_QK - ROPE_DIMS), F32)
    c = jnp.concatenate([cos, cos, ones], axis=1)
    s1 = jnp.concatenate([zeros_h, sin, zeros_r], axis=1)
    s2 = jnp.concatenate([-sin, zeros_h, zeros_r], axis=1)
    return tuple(jnp.concatenate([a, a], axis=1) for a in (c, s1, s2))


def _prepare_weights(norm_gain, w_in, mu_shift, w0, w_up, a0, a_up, k_k, k_a, r_k, gn_gain, gn_bias,
                     q_norm_gain, k_norm_gain, subln_gain, w_out):
    d3 = 3 * D_RWKV
    shift_cols = d3 + 2 * LORA
    w_main = jnp.concatenate([w_in[:, :d3], w_in[:, shift_cols:]], axis=1).astype(BF16)
    w_lora = w_in[:, d3:shift_cols].astype(BF16)
    row = lambda a: a.reshape(1, -1).astype(F32)
    zpad = jnp.zeros((LORA, D_RWKV), F32)
    dirs = []
    for d in range(2):
        dirs.append((
            row(mu_shift[:D_RWKV]), row(mu_shift[D_RWKV:2 * D_RWKV]), row(mu_shift[2 * D_RWKV:d3]),
            row(mu_shift[d3:shift_cols]),
            row(w0[d]), jnp.concatenate([w_up[d], zpad], axis=0).astype(BF16),
            row(a0[d]), jnp.concatenate([zpad, a_up[d]], axis=0).astype(BF16),
            row(k_k), row(k_a)))
    final_extra = (row(r_k), row(gn_gain), row(gn_bias))
    attn = (row(jnp.tile(q_norm_gain, 2)), row(jnp.tile(k_norm_gain, 2)), row(subln_gain))
    w_o = w_out.astype(BF16)
    return row(norm_gain), w_main, w_lora, dirs, final_extra, attn, (w_o[:D_RWKV], w_o[D_RWKV:])


def _layer(x, weights, lambda_qk):
    gain, w_main, w_lora, dirs, final_extra, attn, (w_r, w_d) = weights
    b, t, _ = x.shape
    x2 = x.reshape(b * t, D_MODEL)
    proj2, lora2 = _in_proj(x2, gain, w_main, w_lora)
    proj = proj2.reshape(b, t, D_MAIN)
    lora = lora2.reshape(b, t, 2 * LORA)

    o_f = _rwkv_call(False, False, proj, lora, dirs[0], None)
    y_r = _rwkv_call(True, True, proj, lora, dirs[1], final_extra + (o_f,))

    tk = 512
    cos, s1, s2 = _rope_tables(t)
    q_gain, k_gain, subln = attn
    qt, kn, vt = _attn_prep(proj, q_gain, k_gain, cos, s1, s2, tk)
    y_d = _diff_attn(qt, kn, vt, proj, lambda_qk, subln)

    out = _out_proj(y_r.reshape(b * t, D_RWKV), y_d.reshape(b * t, D_DIFF), w_r, w_d, x2)
    return out.reshape(b, t, D_MODEL)


def kernel(x_prompt, x_sample, norm_gain, w_in, mu_shift, w0, w_up, a0, a_up, k_k, k_a, r_k, gn_gain,
           gn_bias, q_norm_gain, k_norm_gain, lambda_qk, subln_gain, w_out):
    weights = _prepare_weights(norm_gain[0], w_in[0], mu_shift[0], w0[0], w_up[0], a0[0], a_up[0],
                               k_k[0], k_a[0], r_k[0], gn_gain[0], gn_bias[0], q_norm_gain[0],
                               k_norm_gain[0], subln_gain[0], w_out[0])
    lam = lambda_qk[0].astype(F32)
    return (_layer(x_prompt, weights, lam), _layer(x_sample, weights, lam))
```

```python
import functools
import math

import jax
import jax.numpy as jnp
from jax import lax
from jax.experimental import pallas as pl
from jax.experimental.pallas import tpu as pltpu

F32 = jnp.float32
BF16 = jnp.bfloat16

D_MODEL = 2048
D_RWKV = 1024
D_DIFF = 1024
RWKV_HEAD = 64
DIFF_VDIM = 128
DIFF_QK = 64
N_DIFF_HEADS = D_DIFF // DIFF_VDIM
LORA = 64
ROPE_DIMS = DIFF_QK // 4
ROPE_HALF = ROPE_DIMS // 2
ROPE_THETA = 500000.0
RMS_EPS = 1e-6
GN_EPS = 64e-5
DECAY_SCALE = 0.606531
LAMBDA_INIT = 0.8 - 0.6 * math.exp(-0.3 * 0)
LOG2E = 1.4426950408889634

LANES = 128
HALO = 8
BF16_ROWS = 16
CHUNK = 64
PAIRS = D_RWKV // LANES
D_MAIN = 8 * 1024
ATTN_UNROLL = 4
VMEM_LIMIT = 56 * 1024 * 1024

COL_R, COL_K, COL_V, COL_GR, COL_Q, COL_KD, COL_VD, COL_GD = range(8)


def _dot(a, b):
    return jnp.dot(a, b, preferred_element_type=F32)


def _dot_nt(a, b):
    return lax.dot_general(a, b, (((1,), (1,)), ((), ())), preferred_element_type=F32)


def _dot_tn(a, b):
    return lax.dot_general(a, b, (((0,), (0,)), ((), ())), preferred_element_type=F32)


def _split(x):
    hi = x.astype(BF16)
    return hi, (x - hi.astype(F32)).astype(BF16)


def _silu(x):
    h = 0.5 * x
    return h + h * jnp.tanh(h)


def _seg_ones(width):
    r = lax.broadcasted_iota(jnp.int32, (LANES, LANES), 0) // width
    c = lax.broadcasted_iota(jnp.int32, (LANES, LANES), 1) // width
    return (r == c).astype(BF16)


def _in_proj_kernel(x_ref, g_ref, w_ref, wl_ref, o_ref, ol_ref, h_scr):
    @pl.when(pl.program_id(1) == 0)
    def _():
        x = x_ref[...]
        ms = jnp.mean(x * x, axis=-1, keepdims=True)
        h = (x * lax.rsqrt(ms + RMS_EPS) * g_ref[...]).astype(BF16)
        h_scr[...] = h
        ol_ref[...] = _dot(h, wl_ref[...])

    o_ref[...] = _dot(h_scr[...], w_ref[...]).astype(o_ref.dtype)


def _in_proj(x2, gain, w_main, w_lora, tm=1024, tn=1024):
    n = x2.shape[0]
    return pl.pallas_call(
        _in_proj_kernel,
        grid=(n // tm, D_MAIN // tn),
        in_specs=[
            pl.BlockSpec((tm, D_MODEL), lambda i, j: (i, 0)),
            pl.BlockSpec((1, D_MODEL), lambda i, j: (0, 0)),
            pl.BlockSpec((D_MODEL, tn), lambda i, j: (0, j)),
            pl.BlockSpec((D_MODEL, 2 * LORA), lambda i, j: (0, 0)),
        ],
        out_specs=[
            pl.BlockSpec((tm, tn), lambda i, j: (i, j)),
            pl.BlockSpec((tm, 2 * LORA), lambda i, j: (i, 0)),
        ],
        out_shape=[
            jax.ShapeDtypeStruct((n, D_MAIN), BF16),
            jax.ShapeDtypeStruct((n, 2 * LORA), F32),
        ],
        scratch_shapes=[pltpu.VMEM((tm, D_MODEL), BF16)],
        compiler_params=pltpu.CompilerParams(
            dimension_semantics=("arbitrary", "arbitrary"), vmem_limit_bytes=VMEM_LIMIT),
        name="in_proj",
    )(x2, gain, w_main, w_lora)


def _shifted(x, prev_row, next_row, mu, row):
    tb = x.shape[0]
    prev = jnp.where(row == 0, prev_row, pltpu.roll(x, 1, 0))
    nxt = jnp.where(row == tb - 1, next_row, pltpu.roll(x, tb - 1, 0))
    return x + mu * (0.5 * (prev + nxt) - x)


def _rwkv_kernel(rev, final, tb, *refs):
    (r_ref, rp_ref, rn_ref, k_ref, kp_ref, kn_ref, v_ref, vp_ref, vn_ref,
     l_ref, lp_ref, ln_ref, mu_ref, half_ref, mul_ref,
     w0_ref, wup_ref, a0_ref, aup_ref, kk_ref, ka_ref) = refs[:21]
    if final:
        (rk_ref, gng_ref, gnb_ref, g_ref, of_ref, out_ref,
         a_s, r_s, bt_s, kt_s, bh_s, kh_s, v_s, dec_s, ob, bonus_s, state) = refs[21:]
    else:
        out_ref, a_s, r_s, bt_s, kt_s, bh_s, kh_s, v_s, dec_s, state = refs[21:]
        ob = out_ref.at[0]

    i = pl.program_id(1)
    nblk = pl.num_programs(1)
    blk = (nblk - 1 - i) if rev else i
    nchunk = tb // CHUNK

    @pl.when(i == 0)
    def _():
        state[...] = jnp.zeros_like(state)

    has_prev = (blk > 0).astype(F32)
    has_next = (blk < nblk - 1).astype(F32)
    ti = lax.broadcasted_iota(jnp.int32, (tb, tb), 0)
    tj = lax.broadcasted_iota(jnp.int32, (tb, tb), 1)
    delta = (0.5 * ((tj == ti - 1) | (tj == ti + 1)).astype(F32) - (tj == ti).astype(F32)).astype(BF16)
    row8 = lax.broadcasted_iota(jnp.int32, (HALO, 1), 0)
    row = lax.broadcasted_iota(jnp.int32, (tb, 1), 0)
    z = _shifted(l_ref[0], lp_ref[0, HALO - 1:HALO, :] * has_prev, ln_ref[0, 0:1, :] * has_next,
                 mul_ref[...], row)
    lane = lax.broadcasted_iota(jnp.int32, (tb, 2 * LORA), 1)
    zt = jnp.where(lane < LORA, jnp.tanh(z), z).astype(BF16)
    ones64 = _seg_ones(RWKV_HEAD)
    same = (ti // CHUNK) == (tj // CHUNK)
    order = (tj >= ti) if rev else (tj <= ti)
    tri_blk = (same & order).astype(BF16)
    pairs = range(PAIRS)
    lns = [slice(p * LANES, (p + 1) * LANES) for p in pairs]

    def seg(x):
        xb = x.astype(BF16)
        return jnp.concatenate([_dot(xb[:, ln], ones64) for ln in lns], axis=1)

    def shift(col, ref, p_ref, n_ref):
        cl = slice(col * D_RWKV, (col + 1) * D_RWKV)
        half_mu = half_ref[:, cl]
        xb = ref[0]
        y = xb.astype(F32) + _dot(delta, xb) * mu_ref[:, cl]
        top = jnp.where(row8 == 0,
                        p_ref[0, BF16_ROWS - 1:BF16_ROWS, :].astype(F32) * (half_mu * has_prev), 0.0)
        bot = jnp.where(row8 == HALO - 1, n_ref[0, 0:1, :].astype(F32) * (half_mu * has_next), 0.0)
        return jnp.concatenate([y[:HALO] + top, y[HALO:tb - HALO], y[tb - HALO:] + bot], axis=0)

    rr = shift(0, r_ref, rp_ref, rn_ref)
    kk_s = shift(1, k_ref, kp_ref, kn_ref)
    vv = shift(2, v_ref, vp_ref, vn_ref)
    logw = (-0.5 * DECAY_SCALE * LOG2E) * (1.0 + jnp.tanh(w0_ref[...] + _dot(zt, wup_ref[...])))
    asig = 0.5 + 0.5 * jnp.tanh(a0_ref[...] + _dot(zt, aup_ref[...]))
    kk = kk_s * kk_ref[...]
    kk = kk * lax.rsqrt(jnp.maximum(seg(kk * kk), 1e-12))
    kd = kk_s * (1.0 + (asig - 1.0) * ka_ref[...])
    bb = asig * kk
    if final:
        bonus_s[...] = seg(rr * kk_s * rk_ref[...]) * vv

    lw_hi, lw_lo = _split(logw)
    cum = _dot(tri_blk, lw_hi) + _dot(tri_blk, lw_lo)
    g_inv = jnp.exp2(-cum)
    bt = bb * g_inv
    kt = kd * g_inv
    a_s[...] = (-kk * jnp.exp2(cum - logw)).astype(BF16)
    r_s[...] = (rr * jnp.exp2(cum)).astype(BF16)
    bt_s[...] = bt.astype(BF16)
    kt_s[...] = kt.astype(BF16)
    v_s[...] = vv.astype(BF16)
    for c in range(nchunk):
        rows = slice(c * CHUNK, (c + 1) * CHUNK)
        end = c * CHUNK if rev else (c + 1) * CHUNK - 1
        dec = jnp.exp2(cum[end:end + 1, :])
        bh_s[rows, :] = (bt[rows] * dec).astype(BF16)
        kh_s[rows, :] = (kt[rows] * dec).astype(BF16)
        dec_s[c * HALO:(c + 1) * HALO, :] = jnp.broadcast_to(dec, (HALO, D_RWKV))

    t_i = lax.broadcasted_iota(jnp.int32, (CHUNK, LANES), 0)
    j_i = lax.broadcasted_iota(jnp.int32, (CHUNK, LANES), 1) % CHUNK
    if rev:
        strict, incl = j_i > t_i, j_i >= t_i
    else:
        strict, incl = j_i < t_i, j_i <= t_i
    eye_pair = (j_i == t_i).astype(F32)
    bd_mask = (lax.broadcasted_iota(jnp.int32, (LANES, LANES), 0) // CHUNK
               == lax.broadcasted_iota(jnp.int32, (LANES, LANES), 1) // CHUNK)
    zero_b = jnp.zeros((), BF16)

    def bd(xb):
        return jnp.where(bd_mask, jnp.concatenate([xb, xb], axis=0), zero_b)

    def stack(x, y):
        return jnp.concatenate([x, y], axis=0)

    def chunk_body(ci, carry):
        c = (nchunk - 1 - ci) if rev else ci
        rows = pl.ds(pl.multiple_of(c * CHUNK, CHUNK), CHUNK)
        drow = pl.ds(pl.multiple_of(c * HALO, HALO), 1)
        a_t = [a_s[rows, ln] for ln in lns]
        r_t = [r_s[rows, ln] for ln in lns]
        v_c = [v_s[rows, ln] for ln in lns]
        bd_v = [bd(v) for v in v_c]

        q = [_dot_nt(stack(a_t[p], r_t[p]), stack(bd(bt_s[rows, lns[p]]), bd(kt_s[rows, lns[p]])))
             for p in pairs]
        l_ab = [jnp.where(strict, x[:CHUNK, :LANES], 0.0) for x in q]
        l_ak = [jnp.where(strict, x[:CHUNK, LANES:], 0.0).astype(BF16) for x in q]
        m_r = [jnp.concatenate([jnp.where(incl, x[CHUNK:, :LANES], 0.0).astype(BF16),
                                jnp.where(incl, x[CHUNK:, LANES:], 0.0).astype(BF16)], axis=1) for x in q]
        lak_v = [_dot(l_ak[p], bd_v[p]) for p in pairs]

        t_inv = [eye_pair + x for x in l_ab]
        lk = [x.astype(BF16) for x in l_ab]
        lk = [_dot(x, bd(x)).astype(BF16) for x in lk]
        for _ in range(4):
            res = [_dot(stack(lk[p], t_inv[p].astype(BF16)), bd(lk[p])) for p in pairs]
            lk = [x[:CHUNK].astype(BF16) for x in res]
            t_inv = [t_inv[p] + res[p][CHUNK:] for p in pairs]
        t_inv = [(t_inv[p] + _dot(t_inv[p].astype(BF16), bd(lk[p]))).astype(BF16) for p in pairs]

        aw = [_dot(t_inv[p], jnp.concatenate([bd(a_t[p]), bd(lak_v[p].astype(BF16))], axis=1))
              for p in pairs]

        s = [state[p] for p in pairs]
        uo = [_dot_nt(stack(aw[p][:, :LANES].astype(BF16), r_t[p]), s[p].astype(BF16)) for p in pairs]
        u = [(uo[p][:CHUNK] + aw[p][:, LANES:]).astype(BF16) for p in pairs]
        o = [uo[p][CHUNK:] + _dot(m_r[p], stack(bd(u[p]), bd_v[p])) for p in pairs]
        upd = [_dot_tn(stack(u[p], v_c[p]), stack(bh_s[rows, lns[p]], kh_s[rows, lns[p]])) for p in pairs]
        for p in pairs:
            state[p] = s[p] * dec_s[drow, lns[p]] + jnp.where(bd_mask, upd[p], 0.0)
            ob[rows, lns[p]] = o[p]
        return carry

    lax.fori_loop(0, nchunk, chunk_body, 0)

    if final:
        o_all = ob[...] + of_ref[0]
        inv_n = 1.0 / RWKV_HEAD
        cen = o_all - seg(o_all) * inv_n
        var = seg(cen * cen) * inv_n
        on = cen * lax.rsqrt(var + GN_EPS) * gng_ref[...] + gnb_ref[...]
        out_ref[0] = ((on + bonus_s[...]) * _silu(g_ref[0].astype(F32))).astype(out_ref.dtype)


def _rwkv_call(rev, final, proj, lora, params, extra, tb=256):
    b, t, _ = proj.shape
    nblk = t // tb

    def blk(i):
        return (nblk - 1 - i) if rev else i

    def main_spec(col, width=1024):
        return pl.BlockSpec((1, tb, width), lambda bi, i: (bi, blk(i), col))

    def prev_spec(col, width, rows):
        return pl.BlockSpec((1, rows, width),
                            lambda bi, i: (bi, jnp.maximum(blk(i) * (tb // rows) - 1, 0), col))

    def next_spec(col, width, rows):
        return pl.BlockSpec((1, rows, width),
                            lambda bi, i: (bi, jnp.minimum((blk(i) + 1) * (tb // rows), t // rows - 1), col))

    def full_spec(a):
        return pl.BlockSpec(a.shape, lambda bi, i: (0,) * a.ndim)

    in_specs, args = [], []
    for col in (COL_R, COL_K, COL_V):
        in_specs += [main_spec(col), prev_spec(col, 1024, BF16_ROWS), next_spec(col, 1024, BF16_ROWS)]
        args += [proj, proj, proj]
    in_specs += [main_spec(0, 2 * LORA), prev_spec(0, 2 * LORA, HALO), next_spec(0, 2 * LORA, HALO)]
    args += [lora, lora, lora]
    for a in params:
        in_specs.append(full_spec(a))
        args.append(a)
    scratch = [pltpu.VMEM((tb, D_RWKV), BF16) for _ in range(7)]
    scratch.append(pltpu.VMEM((tb // CHUNK * HALO, D_RWKV), F32))
    if final:
        rk, gng, gnb, o_f = extra
        for a in (rk, gng, gnb):
            in_specs.append(full_spec(a))
            args.append(a)
        in_specs += [main_spec(COL_GR), pl.BlockSpec((1, tb, D_RWKV), lambda bi, i: (bi, blk(i), 0))]
        args += [proj, o_f]
        scratch += [pltpu.VMEM((tb, D_RWKV), F32), pltpu.VMEM((tb, D_RWKV), F32)]
        out_dtype = BF16
    else:
        out_dtype = F32
    scratch.append(pltpu.VMEM((PAIRS, LANES, LANES), F32))
    return pl.pallas_call(
        functools.partial(_rwkv_kernel, rev, final, tb),
        grid=(b, nblk),
        in_specs=in_specs,
        out_specs=pl.BlockSpec((1, tb, D_RWKV), lambda bi, i: (bi, blk(i), 0)),
        out_shape=jax.ShapeDtypeStruct((b, t, D_RWKV), out_dtype),
        scratch_shapes=scratch,
        compiler_params=pltpu.CompilerParams(
            dimension_semantics=("arbitrary", "arbitrary"), vmem_limit_bytes=VMEM_LIMIT),
        name="rwkv_bwd" if rev else "rwkv_fwd",
    )(*args)


def _attn_prep_kernel(q_ref, k_ref, v_ref, qg_ref, kg_ref, cos_ref, s1_ref, s2_ref,
                      qt_ref, ko_ref, vt_ref):
    ones64 = _seg_ones(DIFF_QK)
    cos, s1, s2 = cos_ref[...], s1_ref[...], s2_ref[...]

    def norm_rope(x, gain):
        ms = _dot((x * x).astype(BF16), ones64) * (1.0 / DIFF_QK)
        y = x * lax.rsqrt(ms + RMS_EPS) * gain
        return y * cos + pltpu.roll(y, ROPE_HALF, 1) * s1 + pltpu.roll(y, LANES - ROPE_HALF, 1) * s2

    for h in range(N_DIFF_HEADS):
        ln = slice(h * LANES, (h + 1) * LANES)
        qh = norm_rope(q_ref[0, :, ln].astype(F32), qg_ref[...]) * (DIFF_QK ** -0.5 * LOG2E)
        qt_ref[0, ln, :] = qh.T.astype(BF16)
        ko_ref[0, :, ln] = norm_rope(k_ref[0, :, ln].astype(F32), kg_ref[...]).astype(BF16)
        vt_ref[0, 0, ln, :] = v_ref[0, :, ln].astype(F32).T.astype(BF16)


def _attn_prep(proj, q_gain, k_gain, cos, s1, s2, tk):
    b, t, _ = proj.shape
    nk = t // tk
    tab = pl.BlockSpec((tk, LANES), lambda bi, i: (i, 0))
    gain = pl.BlockSpec((1, LANES), lambda bi, i: (0, 0))
    return pl.pallas_call(
        _attn_prep_kernel,
        grid=(b, nk),
        in_specs=[
            pl.BlockSpec((1, tk, D_DIFF), lambda bi, i: (bi, i, COL_Q)),
            pl.BlockSpec((1, tk, D_DIFF), lambda bi, i: (bi, i, COL_KD)),
            pl.BlockSpec((1, tk, D_DIFF), lambda bi, i: (bi, i, COL_VD)),
            gain, gain, tab, tab, tab,
        ],
        out_specs=[
            pl.BlockSpec((1, D_DIFF, tk), lambda bi, i: (bi, 0, i)),
            pl.BlockSpec((1, tk, D_DIFF), lambda bi, i: (bi, i, 0)),
            pl.BlockSpec((1, 1, D_DIFF, tk), lambda bi, i: (bi, i, 0, 0)),
        ],
        out_shape=[
            jax.ShapeDtypeStruct((b, D_DIFF, t), BF16),
            jax.ShapeDtypeStruct((b, t, D_DIFF), BF16),
            jax.ShapeDtypeStruct((b, nk, D_DIFF, tk), BF16),
        ],
        compiler_params=pltpu.CompilerParams(
            dimension_semantics=("arbitrary", "arbitrary"), vmem_limit_bytes=VMEM_LIMIT),
        name="attn_prep",
    )(proj, proj, proj, q_gain, k_gain, cos, s1, s2)


def _diff_attn_kernel(nk, qt_ref, k_ref, vt_ref, g_ref, lam_ref, sub_ref, o_ref,
                      sa_ref, sb_ref, pa_ref, pb_ref):
    qt = qt_ref[0]
    tq = qt.shape[1]
    tk = vt_ref.shape[3]
    sub = lax.broadcasted_iota(jnp.int32, (LANES, tq), 0)
    zero = jnp.zeros((), BF16)
    qq = jnp.concatenate([jnp.where(sub < DIFF_QK, qt, zero), jnp.where(sub >= DIFF_QK, qt, zero)],
                         axis=1)

    def scores(kb, s_ref):
        kblk = k_ref[0, pl.ds(pl.multiple_of(kb * tk, tk), tk), :]
        s_ref[...] = _dot(kblk, qq)

    def softmax_step(s_ref, p_ref, m, l):
        ms, ls, als = [], [], []
        row_chunks = [slice(r * LANES, (r + 1) * LANES) for r in range(tk // LANES)]
        for j in range(2 * tq // LANES):
            ln = slice(j * LANES, (j + 1) * LANES)
            m_new = m[:, ln]
            for rc in row_chunks:
                m_new = jnp.maximum(m_new, jnp.max(s_ref[rc, ln], axis=0, keepdims=True))
            alpha = jnp.exp2(m[:, ln] - m_new)
            l_new = alpha * l[:, ln]
            for rc in row_chunks:
                p = jnp.exp2(s_ref[rc, ln] - m_new)
                p_ref[rc, ln] = p.astype(BF16)
                l_new = l_new + jnp.sum(p, axis=0, keepdims=True)
            ms.append(m_new)
            als.append(alpha)
            ls.append(l_new)
        cat = lambda xs: jnp.concatenate(xs, axis=1)
        return cat(als), cat(ms), cat(ls)

    m = jnp.full((1, 2 * tq), -jnp.inf, F32)
    l = jnp.zeros((1, 2 * tq), F32)
    acc = jnp.zeros((LANES, 2 * tq), F32)
    scores(0, sa_ref)
    al_a, m, l = softmax_step(sa_ref, pa_ref, m, l)
    scores(1, sb_ref)

    def stage(kb, acc, alpha, m, l, s_mine, p_mine, s_other, p_other, last):
        if not last:
            scores(kb + 2, s_mine)
        pv = _dot(vt_ref[0, kb], p_mine[...])
        alpha_other, m, l = softmax_step(s_other, p_other, m, l)
        return alpha * acc + pv, alpha_other, m, l

    bufs = ((sa_ref, pa_ref), (sb_ref, pb_ref))

    def run_stages(kb0, count, carry, tail):
        m, l, acc, alpha = carry
        for u in range(count):
            (s_mine, p_mine), (s_other, p_other) = bufs[u % 2], bufs[(u + 1) % 2]
            acc, alpha, m, l = stage(kb0 + u, acc, alpha, m, l, s_mine, p_mine, s_other, p_other,
                                     tail and u >= count - 1)
        return m, l, acc, alpha

    n_loop = (nk - 2) // ATTN_UNROLL
    carry = lax.fori_loop(0, n_loop, lambda i, c: run_stages(ATTN_UNROLL * i, ATTN_UNROLL, c, False),
                          (m, l, acc, al_a))
    done = n_loop * ATTN_UNROLL
    m, l, acc, alpha = run_stages(done, nk - 1 - done, carry, True)
    acc = alpha * acc + _dot(vt_ref[0, nk - 1], bufs[(nk - 1) % 2][1][...])
    on = acc * (1.0 / l)
    lq = lam_ref[...]
    lam = (jnp.exp(jnp.sum(lq[0:1] * lq[1:2], axis=-1, keepdims=True))
           - jnp.exp(jnp.sum(lq[2:3] * lq[3:4], axis=-1, keepdims=True)) + LAMBDA_INIT)
    o = (on[:, :tq] - lam * on[:, tq:]).T
    ms = jnp.mean(o * o, axis=-1, keepdims=True)
    y = o * lax.rsqrt(ms + RMS_EPS) * sub_ref[...] * (1.0 - LAMBDA_INIT)
    o_ref[0] = (y * _silu(g_ref[0].astype(F32))).astype(o_ref.dtype)


def _diff_attn(qt, kn, vt, proj, lambda_qk, subln, tq=512):
    b, _, t = qt.shape
    nk, tk = vt.shape[1], vt.shape[3]
    assert nk >= 2 and nk % 2 == 0
    gcol = COL_GD * (1024 // LANES)
    return pl.pallas_call(
        functools.partial(_diff_attn_kernel, nk),
        grid=(b, N_DIFF_HEADS, t // tq),
        in_specs=[
            pl.BlockSpec((1, LANES, tq), lambda bi, h, i: (bi, h, i)),
            pl.BlockSpec((1, t, LANES), lambda bi, h, i: (bi, 0, h)),
            pl.BlockSpec((1, nk, LANES, tk), lambda bi, h, i: (bi, 0, h, 0)),
            pl.BlockSpec((1, tq, LANES), lambda bi, h, i: (bi, i, gcol + h)),
            pl.BlockSpec((4, DIFF_QK), lambda bi, h, i: (0, 0)),
            pl.BlockSpec((1, LANES), lambda bi, h, i: (0, 0)),
        ],
        out_specs=pl.BlockSpec((1, tq, LANES), lambda bi, h, i: (bi, i, h)),
        out_shape=jax.ShapeDtypeStruct((b, t, D_DIFF), BF16),
        scratch_shapes=[pltpu.VMEM((tk, 2 * tq), F32), pltpu.VMEM((tk, 2 * tq), F32),
                        pltpu.VMEM((tk, 2 * tq), BF16), pltpu.VMEM((tk, 2 * tq), BF16)],
        compiler_params=pltpu.CompilerParams(
            dimension_semantics=("arbitrary", "arbitrary", "arbitrary"), vmem_limit_bytes=VMEM_LIMIT),
        name="diff_attn",
    )(qt, kn, vt, proj, lambda_qk, subln)


def _out_proj_kernel(yr_ref, yd_ref, wr_ref, wd_ref, x_ref, o_ref):
    o_ref[...] = x_ref[...] + _dot(yr_ref[...], wr_ref[...]) + _dot(yd_ref[...], wd_ref[...])


def _out_proj(y_r, y_d, w_o, x2, tm=512, tn=1024):
    n = x2.shape[0]
    return pl.pallas_call(
        _out_proj_kernel,
        grid=(n // tm, D_MODEL // tn),
        in_specs=[
            pl.BlockSpec((tm, D_RWKV), lambda i, j: (i, 0)),
            pl.BlockSpec((tm, D_DIFF), lambda i, j: (i, 0)),
            pl.BlockSpec((D_RWKV, tn), lambda i, j: (0, j)),
            pl.BlockSpec((D_DIFF, tn), lambda i, j: (1, j)),
            pl.BlockSpec((tm, tn), lambda i, j: (i, j)),
        ],
        out_specs=pl.BlockSpec((tm, tn), lambda i, j: (i, j)),
        out_shape=jax.ShapeDtypeStruct((n, D_MODEL), F32),
        compiler_params=pltpu.CompilerParams(
            dimension_semantics=("arbitrary", "arbitrary"), vmem_limit_bytes=VMEM_LIMIT),
        name="out_proj",
    )(y_r, y_d, w_o, w_o, x2)


def _rope_tables(t):
    inv = ROPE_THETA ** (-jnp.arange(ROPE_HALF, dtype=F32) * 2.0 / ROPE_DIMS)
    ang = jnp.arange(t, dtype=F32)[:, None] * inv[None, :]
    lane = jnp.arange(LANES) % DIFF_QK
    freq = jnp.arange(ROPE_HALF)[:, None]
    lo = (lane[None, :] == freq).astype(F32)
    hi = (lane[None, :] == freq + ROPE_HALF).astype(F32)
    expand = functools.partial(jnp.dot, precision=lax.Precision.HIGHEST)
    cos, sin = jnp.cos(ang), jnp.sin(ang)
    c = expand(cos, lo + hi) + (lane >= ROPE_DIMS).astype(F32)[None, :]
    return c, expand(sin, hi), expand(-sin, lo)


def _prepare_weights(norm_gain, w_in, mu_shift, w0, w_up, a0, a_up, k_k, k_a, r_k, gn_gain, gn_bias,
                     q_norm_gain, k_norm_gain, subln_gain, w_out):
    d3 = 3 * D_RWKV
    shift_cols = d3 + 2 * LORA
    w_main = jnp.concatenate([w_in[:, :d3], w_in[:, shift_cols:]], axis=1).astype(BF16)
    w_lora = w_in[:, d3:shift_cols].astype(BF16)
    row = lambda a: a.reshape(1, -1).astype(F32)
    zpad = jnp.zeros((LORA, D_RWKV), F32)
    dirs = []
    for d in range(2):
        dirs.append((
            row(mu_shift[:d3]), row(0.5 * mu_shift[:d3]), row(mu_shift[d3:shift_cols]),
            row(0.5 * w0[d]), jnp.concatenate([0.5 * w_up[d], zpad], axis=0).astype(BF16),
            row(0.5 * a0[d]), jnp.concatenate([zpad, 0.5 * a_up[d]], axis=0).astype(BF16),
            row(k_k), row(k_a)))
    final_extra = (row(r_k), row(gn_gain), row(gn_bias))
    attn = (row(jnp.tile(q_norm_gain, 2)), row(jnp.tile(k_norm_gain, 2)), row(subln_gain))
    return row(norm_gain), w_main, w_lora, dirs, final_extra, attn, w_out.astype(BF16)


def _layer(x, weights, lambda_qk, rope):
    gain, w_main, w_lora, dirs, final_extra, attn, w_o = weights
    b, t, _ = x.shape
    x2 = x.reshape(b * t, D_MODEL)
    proj2, lora2 = _in_proj(x2, gain, w_main, w_lora)
    proj = proj2.reshape(b, t, D_MAIN)
    lora = lora2.reshape(b, t, 2 * LORA)

    o_f = _rwkv_call(False, False, proj, lora, dirs[0], None)
    y_r = _rwkv_call(True, True, proj, lora, dirs[1], final_extra + (o_f,))

    tk = 512
    cos, s1, s2 = rope
    q_gain, k_gain, subln = attn
    qt, kn, vt = _attn_prep(proj, q_gain, k_gain, cos, s1, s2, tk)
    y_d = _diff_attn(qt, kn, vt, proj, lambda_qk, subln)

    out = _out_proj(y_r.reshape(b * t, D_RWKV), y_d.reshape(b * t, D_DIFF), w_o, x2)
    return out.reshape(b, t, D_MODEL)


def kernel(x_prompt, x_sample, norm_gain, w_in, mu_shift, w0, w_up, a0, a_up, k_k, k_a, r_k, gn_gain,
           gn_bias, q_norm_gain, k_norm_gain, lambda_qk, subln_gain, w_out):
    weights = _prepare_weights(norm_gain[0], w_in[0], mu_shift[0], w0[0], w_up[0], a0[0], a_up[0],
                               k_k[0], k_a[0], r_k[0], gn_gain[0], gn_bias[0], q_norm_gain[0],
                               k_norm_gain[0], subln_gain[0], w_out[0])
    lam = lambda_qk[0].astype(F32)
    rope = _rope_tables(max(x_prompt.shape[1], x_sample.shape[1]))
    return (_layer(x_prompt, weights, lam, rope), _layer(x_sample, weights, lam, rope))
```

```python
import functools
import math

import jax
import jax.numpy as jnp
from jax import lax
from jax.experimental import pallas as pl
from jax.experimental.pallas import tpu as pltpu

F32 = jnp.float32
BF16 = jnp.bfloat16

D_MODEL = 2048
D_RWKV = 1024
D_DIFF = 1024
RWKV_HEAD = 64
DIFF_VDIM = 128
DIFF_QK = 64
N_DIFF_HEADS = D_DIFF // DIFF_VDIM
LORA = 64
ROPE_DIMS = DIFF_QK // 4
ROPE_HALF = ROPE_DIMS // 2
ROPE_THETA = 500000.0
RMS_EPS = 1e-6
GN_EPS = 64e-5
DECAY_SCALE = 0.606531
LAMBDA_INIT = 0.8 - 0.6 * math.exp(-0.3 * 0)
LOG2E = 1.4426950408889634

LANES = 128
HALO = 8
BF16_ROWS = 16
CHUNK = 64
PAIRS = D_RWKV // LANES
D_MAIN = 8 * 1024
ATTN_UNROLL = 4
ATTN_STRIP = 256
ATTN_BLOCKS = 4
SCORE_BOUND_COEF = 1.01 * DIFF_QK * DIFF_QK ** -0.5 * LOG2E
ATTN_BOUND_MAX = 40.0
VMEM_LIMIT = 56 * 1024 * 1024

COL_R, COL_K, COL_V, COL_GR, COL_Q, COL_KD, COL_VD, COL_GD = range(8)


def _dot(a, b):
    return jnp.dot(a, b, preferred_element_type=F32)


def _dot_nt(a, b):
    return lax.dot_general(a, b, (((1,), (1,)), ((), ())), preferred_element_type=F32)


def _dot_tn(a, b):
    return lax.dot_general(a, b, (((0,), (0,)), ((), ())), preferred_element_type=F32)


def _split(x):
    hi = x.astype(BF16)
    return hi, (x - hi.astype(F32)).astype(BF16)


def _silu(x):
    h = 0.5 * x
    return h + h * jnp.tanh(h)


def _seg_ones(width):
    r = lax.broadcasted_iota(jnp.int32, (LANES, LANES), 0) // width
    c = lax.broadcasted_iota(jnp.int32, (LANES, LANES), 1) // width
    return (r == c).astype(BF16)


def _in_proj_kernel(x_ref, g_ref, w_ref, wl_ref, o_ref, ol_ref, h_scr):
    @pl.when(pl.program_id(1) == 0)
    def _():
        x = x_ref[...]
        ms = jnp.mean(x * x, axis=-1, keepdims=True)
        h = (x * lax.rsqrt(ms + RMS_EPS) * g_ref[...]).astype(BF16)
        h_scr[...] = h
        ol_ref[...] = _dot(h, wl_ref[...])

    o_ref[...] = _dot(h_scr[...], w_ref[...]).astype(o_ref.dtype)


def _in_proj(x2, gain, w_main, w_lora, tm=1024, tn=1024):
    n = x2.shape[0]
    return pl.pallas_call(
        _in_proj_kernel,
        grid=(n // tm, D_MAIN // tn),
        in_specs=[
            pl.BlockSpec((tm, D_MODEL), lambda i, j: (i, 0)),
            pl.BlockSpec((1, D_MODEL), lambda i, j: (0, 0)),
            pl.BlockSpec((D_MODEL, tn), lambda i, j: (0, j)),
            pl.BlockSpec((D_MODEL, 2 * LORA), lambda i, j: (0, 0)),
        ],
        out_specs=[
            pl.BlockSpec((tm, tn), lambda i, j: (i, j)),
            pl.BlockSpec((tm, 2 * LORA), lambda i, j: (i, 0)),
        ],
        out_shape=[
            jax.ShapeDtypeStruct((n, D_MAIN), BF16),
            jax.ShapeDtypeStruct((n, 2 * LORA), F32),
        ],
        scratch_shapes=[pltpu.VMEM((tm, D_MODEL), BF16)],
        compiler_params=pltpu.CompilerParams(
            dimension_semantics=("arbitrary", "arbitrary"), vmem_limit_bytes=VMEM_LIMIT),
        name="in_proj",
    )(x2, gain, w_main, w_lora)


def _shifted(x, prev_row, next_row, mu, row):
    tb = x.shape[0]
    prev = jnp.where(row == 0, prev_row, pltpu.roll(x, 1, 0))
    nxt = jnp.where(row == tb - 1, next_row, pltpu.roll(x, tb - 1, 0))
    return x + mu * (0.5 * (prev + nxt) - x)


def _rwkv_kernel(rev, final, tb, *refs):
    (r_ref, rp_ref, rn_ref, k_ref, kp_ref, kn_ref, v_ref, vp_ref, vn_ref,
     l_ref, lp_ref, ln_ref, mu_ref, half_ref, mul_ref,
     w0_ref, wup_ref, a0_ref, aup_ref, kk_ref, ka_ref) = refs[:21]
    if final:
        (rk_ref, gng_ref, gnb_ref, g_ref, of_ref, out_ref,
         a_s, r_s, bt_s, kt_s, bh_s, kh_s, v_s, dec_s, ob, bonus_s, state) = refs[21:]
    else:
        out_ref, a_s, r_s, bt_s, kt_s, bh_s, kh_s, v_s, dec_s, state = refs[21:]
        ob = out_ref.at[0]

    i = pl.program_id(1)
    nblk = pl.num_programs(1)
    blk = (nblk - 1 - i) if rev else i
    nchunk = tb // CHUNK

    @pl.when(i == 0)
    def _():
        state[...] = jnp.zeros_like(state)

    has_prev = (blk > 0).astype(F32)
    has_next = (blk < nblk - 1).astype(F32)
    ti = lax.broadcasted_iota(jnp.int32, (tb, tb), 0)
    tj = lax.broadcasted_iota(jnp.int32, (tb, tb), 1)
    delta = (0.5 * ((tj == ti - 1) | (tj == ti + 1)).astype(F32) - (tj == ti).astype(F32)).astype(BF16)
    row8 = lax.broadcasted_iota(jnp.int32, (HALO, 1), 0)
    row = lax.broadcasted_iota(jnp.int32, (tb, 1), 0)
    z = _shifted(l_ref[0], lp_ref[0, HALO - 1:HALO, :] * has_prev, ln_ref[0, 0:1, :] * has_next,
                 mul_ref[...], row)
    lane = lax.broadcasted_iota(jnp.int32, (tb, 2 * LORA), 1)
    zt = jnp.where(lane < LORA, jnp.tanh(z), z).astype(BF16)
    ones64 = _seg_ones(RWKV_HEAD)
    same = (ti // CHUNK) == (tj // CHUNK)
    order = (tj >= ti) if rev else (tj <= ti)
    tri_blk = (same & order).astype(BF16)
    pairs = range(PAIRS)
    lns = [slice(p * LANES, (p + 1) * LANES) for p in pairs]

    def seg(x):
        xb = x.astype(BF16)
        return jnp.concatenate([_dot(xb[:, ln], ones64) for ln in lns], axis=1)

    def shift(col, ref, p_ref, n_ref):
        cl = slice(col * D_RWKV, (col + 1) * D_RWKV)
        half_mu = half_ref[:, cl]
        xb = ref[0]
        y = xb.astype(F32) + _dot(delta, xb) * mu_ref[:, cl]
        top = jnp.where(row8 == 0,
                        p_ref[0, BF16_ROWS - 1:BF16_ROWS, :].astype(F32) * (half_mu * has_prev), 0.0)
        bot = jnp.where(row8 == HALO - 1, n_ref[0, 0:1, :].astype(F32) * (half_mu * has_next), 0.0)
        return jnp.concatenate([y[:HALO] + top, y[HALO:tb - HALO], y[tb - HALO:] + bot], axis=0)

    rr = shift(0, r_ref, rp_ref, rn_ref)
    kk_s = shift(1, k_ref, kp_ref, kn_ref)
    vv = shift(2, v_ref, vp_ref, vn_ref)
    logw = (-0.5 * DECAY_SCALE * LOG2E) * (1.0 + jnp.tanh(w0_ref[...] + _dot(zt, wup_ref[...])))
    asig = 0.5 + 0.5 * jnp.tanh(a0_ref[...] + _dot(zt, aup_ref[...]))
    kk = kk_s * kk_ref[...]
    kk = kk * lax.rsqrt(jnp.maximum(seg(kk * kk), 1e-12))
    kd = kk_s * (1.0 + (asig - 1.0) * ka_ref[...])
    bb = asig * kk
    if final:
        bonus_s[...] = seg(rr * kk_s * rk_ref[...]) * vv

    lw_hi, lw_lo = _split(logw)
    cum = _dot(tri_blk, lw_hi) + _dot(tri_blk, lw_lo)
    g_inv = jnp.exp2(-cum)
    bt = bb * g_inv
    kt = kd * g_inv
    a_s[...] = (-kk * jnp.exp2(cum - logw)).astype(BF16)
    r_s[...] = (rr * jnp.exp2(cum)).astype(BF16)
    bt_s[...] = bt.astype(BF16)
    kt_s[...] = kt.astype(BF16)
    v_s[...] = vv.astype(BF16)
    for c in range(nchunk):
        rows = slice(c * CHUNK, (c + 1) * CHUNK)
        end = c * CHUNK if rev else (c + 1) * CHUNK - 1
        dec = jnp.exp2(cum[end:end + 1, :])
        bh_s[rows, :] = (bt[rows] * dec).astype(BF16)
        kh_s[rows, :] = (kt[rows] * dec).astype(BF16)
        dec_s[c * HALO:(c + 1) * HALO, :] = jnp.broadcast_to(dec, (HALO, D_RWKV))

    t_i = lax.broadcasted_iota(jnp.int32, (CHUNK, LANES), 0)
    j_i = lax.broadcasted_iota(jnp.int32, (CHUNK, LANES), 1) % CHUNK
    if rev:
        strict, incl = j_i > t_i, j_i >= t_i
    else:
        strict, incl = j_i < t_i, j_i <= t_i
    eye_pair = (j_i == t_i).astype(F32)
    bd_mask = (lax.broadcasted_iota(jnp.int32, (LANES, LANES), 0) // CHUNK
               == lax.broadcasted_iota(jnp.int32, (LANES, LANES), 1) // CHUNK)
    zero_b = jnp.zeros((), BF16)

    def bd(xb):
        return jnp.where(bd_mask, jnp.concatenate([xb, xb], axis=0), zero_b)

    def stack(x, y):
        return jnp.concatenate([x, y], axis=0)

    def chunk_body(ci, carry):
        c = (nchunk - 1 - ci) if rev else ci
        rows = pl.ds(pl.multiple_of(c * CHUNK, CHUNK), CHUNK)
        drow = pl.ds(pl.multiple_of(c * HALO, HALO), 1)
        a_t = [a_s[rows, ln] for ln in lns]
        r_t = [r_s[rows, ln] for ln in lns]
        v_c = [v_s[rows, ln] for ln in lns]
        bd_v = [bd(v) for v in v_c]

        q = [_dot_nt(stack(a_t[p], r_t[p]), stack(bd(bt_s[rows, lns[p]]), bd(kt_s[rows, lns[p]])))
             for p in pairs]
        l_ab = [jnp.where(strict, x[:CHUNK, :LANES], 0.0) for x in q]
        l_ak = [jnp.where(strict, x[:CHUNK, LANES:], 0.0).astype(BF16) for x in q]
        m_r = [jnp.concatenate([jnp.where(incl, x[CHUNK:, :LANES], 0.0).astype(BF16),
                                jnp.where(incl, x[CHUNK:, LANES:], 0.0).astype(BF16)], axis=1) for x in q]
        lak_v = [_dot(l_ak[p], bd_v[p]) for p in pairs]

        t_inv = [eye_pair + x for x in l_ab]
        lk = [x.astype(BF16) for x in l_ab]
        lk = [_dot(x, bd(x)).astype(BF16) for x in lk]
        for _ in range(4):
            res = [_dot(stack(lk[p], t_inv[p].astype(BF16)), bd(lk[p])) for p in pairs]
            lk = [x[:CHUNK].astype(BF16) for x in res]
            t_inv = [t_inv[p] + res[p][CHUNK:] for p in pairs]
        t_inv = [(t_inv[p] + _dot(t_inv[p].astype(BF16), bd(lk[p]))).astype(BF16) for p in pairs]

        aw = [_dot(t_inv[p], jnp.concatenate([bd(a_t[p]), bd(lak_v[p].astype(BF16))], axis=1))
              for p in pairs]

        s = [state[p] for p in pairs]
        uo = [_dot_nt(stack(aw[p][:, :LANES].astype(BF16), r_t[p]), s[p].astype(BF16)) for p in pairs]
        u = [(uo[p][:CHUNK] + aw[p][:, LANES:]).astype(BF16) for p in pairs]
        o = [uo[p][CHUNK:] + _dot(m_r[p], stack(bd(u[p]), bd_v[p])) for p in pairs]
        upd = [_dot_tn(stack(u[p], v_c[p]), stack(bh_s[rows, lns[p]], kh_s[rows, lns[p]])) for p in pairs]
        for p in pairs:
            state[p] = s[p] * dec_s[drow, lns[p]] + jnp.where(bd_mask, upd[p], 0.0)
            ob[rows, lns[p]] = o[p]
        return carry

    lax.fori_loop(0, nchunk, chunk_body, 0)

    if final:
        o_all = ob[...] + of_ref[0]
        inv_n = 1.0 / RWKV_HEAD
        cen = o_all - seg(o_all) * inv_n
        var = seg(cen * cen) * inv_n
        on = cen * lax.rsqrt(var + GN_EPS) * gng_ref[...] + gnb_ref[...]
        out_ref[0] = ((on + bonus_s[...]) * _silu(g_ref[0].astype(F32))).astype(out_ref.dtype)


def _rwkv_call(rev, final, proj, lora, params, extra, tb=256):
    b, t, _ = proj.shape
    nblk = t // tb

    def blk(i):
        return (nblk - 1 - i) if rev else i

    def main_spec(col, width=1024):
        return pl.BlockSpec((1, tb, width), lambda bi, i: (bi, blk(i), col))

    def prev_spec(col, width, rows):
        return pl.BlockSpec((1, rows, width),
                            lambda bi, i: (bi, jnp.maximum(blk(i) * (tb // rows) - 1, 0), col))

    def next_spec(col, width, rows):
        return pl.BlockSpec((1, rows, width),
                            lambda bi, i: (bi, jnp.minimum((blk(i) + 1) * (tb // rows), t // rows - 1), col))

    def full_spec(a):
        return pl.BlockSpec(a.shape, lambda bi, i: (0,) * a.ndim)

    in_specs, args = [], []
    for col in (COL_R, COL_K, COL_V):
        in_specs += [main_spec(col), prev_spec(col, 1024, BF16_ROWS), next_spec(col, 1024, BF16_ROWS)]
        args += [proj, proj, proj]
    in_specs += [main_spec(0, 2 * LORA), prev_spec(0, 2 * LORA, HALO), next_spec(0, 2 * LORA, HALO)]
    args += [lora, lora, lora]
    for a in params:
        in_specs.append(full_spec(a))
        args.append(a)
    scratch = [pltpu.VMEM((tb, D_RWKV), BF16) for _ in range(7)]
    scratch.append(pltpu.VMEM((tb // CHUNK * HALO, D_RWKV), F32))
    if final:
        rk, gng, gnb, o_f = extra
        for a in (rk, gng, gnb):
            in_specs.append(full_spec(a))
            args.append(a)
        in_specs += [main_spec(COL_GR), pl.BlockSpec((1, tb, D_RWKV), lambda bi, i: (bi, blk(i), 0))]
        args += [proj, o_f]
        scratch += [pltpu.VMEM((tb, D_RWKV), F32), pltpu.VMEM((tb, D_RWKV), F32)]
        out_dtype = BF16
    else:
        out_dtype = F32
    scratch.append(pltpu.VMEM((PAIRS, LANES, LANES), F32))
    return pl.pallas_call(
        functools.partial(_rwkv_kernel, rev, final, tb),
        grid=(b, nblk),
        in_specs=in_specs,
        out_specs=pl.BlockSpec((1, tb, D_RWKV), lambda bi, i: (bi, blk(i), 0)),
        out_shape=jax.ShapeDtypeStruct((b, t, D_RWKV), out_dtype),
        scratch_shapes=scratch,
        compiler_params=pltpu.CompilerParams(
            dimension_semantics=("arbitrary", "arbitrary"), vmem_limit_bytes=VMEM_LIMIT),
        name="rwkv_bwd" if rev else "rwkv_fwd",
    )(*args)


def _attn_prep_kernel(q_ref, k_ref, v_ref, qg_ref, kg_ref, cos_ref, s1_ref, s2_ref,
                      qt_ref, ko_ref, vt_ref):
    ones64 = _seg_ones(DIFF_QK)
    cos, s1, s2 = cos_ref[...], s1_ref[...], s2_ref[...]

    def norm_rope(x, gain):
        ms = _dot((x * x).astype(BF16), ones64) * (1.0 / DIFF_QK)
        y = x * lax.rsqrt(ms + RMS_EPS) * gain
        return y * cos + pltpu.roll(y, ROPE_HALF, 1) * s1 + pltpu.roll(y, LANES - ROPE_HALF, 1) * s2

    for h in range(N_DIFF_HEADS):
        ln = slice(h * LANES, (h + 1) * LANES)
        qh = norm_rope(q_ref[0, :, ln].astype(F32), qg_ref[...]) * (DIFF_QK ** -0.5 * LOG2E)
        qt_ref[0, ln, :] = qh.T.astype(BF16)
        ko_ref[0, :, ln] = norm_rope(k_ref[0, :, ln].astype(F32), kg_ref[...]).astype(BF16)
        vt_ref[0, 0, ln, :] = v_ref[0, :, ln].astype(F32).T.astype(BF16)


def _attn_prep(proj, q_gain, k_gain, cos, s1, s2, tk):
    b, t, _ = proj.shape
    nk = t // tk
    tab = pl.BlockSpec((tk, LANES), lambda bi, i: (i, 0))
    gain = pl.BlockSpec((1, LANES), lambda bi, i: (0, 0))
    return pl.pallas_call(
        _attn_prep_kernel,
        grid=(b, nk),
        in_specs=[
            pl.BlockSpec((1, tk, D_DIFF), lambda bi, i: (bi, i, COL_Q)),
            pl.BlockSpec((1, tk, D_DIFF), lambda bi, i: (bi, i, COL_KD)),
            pl.BlockSpec((1, tk, D_DIFF), lambda bi, i: (bi, i, COL_VD)),
            gain, gain, tab, tab, tab,
        ],
        out_specs=[
            pl.BlockSpec((1, D_DIFF, tk), lambda bi, i: (bi, 0, i)),
            pl.BlockSpec((1, tk, D_DIFF), lambda bi, i: (bi, i, 0)),
            pl.BlockSpec((1, 1, D_DIFF, tk), lambda bi, i: (bi, i, 0, 0)),
        ],
        out_shape=[
            jax.ShapeDtypeStruct((b, D_DIFF, t), BF16),
            jax.ShapeDtypeStruct((b, t, D_DIFF), BF16),
            jax.ShapeDtypeStruct((b, nk, D_DIFF, tk), BF16),
        ],
        compiler_params=pltpu.CompilerParams(
            dimension_semantics=("arbitrary", "arbitrary"), vmem_limit_bytes=VMEM_LIMIT),
        name="attn_prep",
    )(proj, proj, proj, q_gain, k_gain, cos, s1, s2)


def _attn_epilogue(acc, l, tq, g_ref, lam_ref, sub_ref, o_ref):
    on = acc * (1.0 / l)
    lq = lam_ref[...]
    lam = (jnp.exp(jnp.sum(lq[0:1] * lq[1:2], axis=-1, keepdims=True))
           - jnp.exp(jnp.sum(lq[2:3] * lq[3:4], axis=-1, keepdims=True)) + LAMBDA_INIT)
    o = (on[:, :tq] - lam * on[:, tq:]).T
    ms = jnp.mean(o * o, axis=-1, keepdims=True)
    y = o * lax.rsqrt(ms + RMS_EPS) * sub_ref[...] * (1.0 - LAMBDA_INIT)
    o_ref[0] = (y * _silu(g_ref[0].astype(F32))).astype(o_ref.dtype)


def _diff_attn_kernel(nk, qt_ref, k_ref, vt_ref, g_ref, lam_ref, sub_ref, o_ref,
                      sa_ref, sb_ref, pa_ref, pb_ref):
    qt = qt_ref[0]
    tq = qt.shape[1]
    tk = vt_ref.shape[3]
    sub = lax.broadcasted_iota(jnp.int32, (LANES, tq), 0)
    zero = jnp.zeros((), BF16)
    qq = jnp.concatenate([jnp.where(sub < DIFF_QK, qt, zero), jnp.where(sub >= DIFF_QK, qt, zero)],
                         axis=1)

    def scores(kb, s_ref):
        kblk = k_ref[0, pl.ds(pl.multiple_of(kb * tk, tk), tk), :]
        s_ref[...] = _dot(kblk, qq)

    def softmax_step(s_ref, p_ref, m, l):
        ms, ls, als = [], [], []
        row_chunks = [slice(r * LANES, (r + 1) * LANES) for r in range(tk // LANES)]
        for j in range(2 * tq // LANES):
            ln = slice(j * LANES, (j + 1) * LANES)
            m_new = m[:, ln]
            for rc in row_chunks:
                m_new = jnp.maximum(m_new, jnp.max(s_ref[rc, ln], axis=0, keepdims=True))
            alpha = jnp.exp2(m[:, ln] - m_new)
            l_new = alpha * l[:, ln]
            for rc in row_chunks:
                p = jnp.exp2(s_ref[rc, ln] - m_new)
                p_ref[rc, ln] = p.astype(BF16)
                l_new = l_new + jnp.sum(p, axis=0, keepdims=True)
            ms.append(m_new)
            als.append(alpha)
            ls.append(l_new)
        cat = lambda xs: jnp.concatenate(xs, axis=1)
        return cat(als), cat(ms), cat(ls)

    m = jnp.full((1, 2 * tq), -jnp.inf, F32)
    l = jnp.zeros((1, 2 * tq), F32)
    acc = jnp.zeros((LANES, 2 * tq), F32)
    scores(0, sa_ref)
    al_a, m, l = softmax_step(sa_ref, pa_ref, m, l)
    scores(1, sb_ref)

    def stage(kb, acc, alpha, m, l, s_mine, p_mine, s_other, p_other, last):
        if not last:
            scores(kb + 2, s_mine)
        pv = _dot(vt_ref[0, kb], p_mine[...])
        alpha_other, m, l = softmax_step(s_other, p_other, m, l)
        return alpha * acc + pv, alpha_other, m, l

    bufs = ((sa_ref, pa_ref), (sb_ref, pb_ref))

    def run_stages(kb0, count, carry, tail):
        m, l, acc, alpha = carry
        for u in range(count):
            (s_mine, p_mine), (s_other, p_other) = bufs[u % 2], bufs[(u + 1) % 2]
            acc, alpha, m, l = stage(kb0 + u, acc, alpha, m, l, s_mine, p_mine, s_other, p_other,
                                     tail and u >= count - 1)
        return m, l, acc, alpha

    n_loop = (nk - 2) // ATTN_UNROLL
    carry = lax.fori_loop(0, n_loop, lambda i, c: run_stages(ATTN_UNROLL * i, ATTN_UNROLL, c, False),
                          (m, l, acc, al_a))
    done = n_loop * ATTN_UNROLL
    m, l, acc, alpha = run_stages(done, nk - 1 - done, carry, True)
    acc = alpha * acc + _dot(vt_ref[0, nk - 1], bufs[(nk - 1) % 2][1][...])
    _attn_epilogue(acc, l, tq, g_ref, lam_ref, sub_ref, o_ref)


def _diff_attn_bounded_kernel(nk, qt_ref, k_ref, vt_ref, g_ref, lam_ref, sub_ref, bound_ref, o_ref):
    qt = qt_ref[0]
    tq = qt.shape[1]
    tk = vt_ref.shape[3]
    sub = lax.broadcasted_iota(jnp.int32, (LANES, tq), 0)
    zero = jnp.zeros((), BF16)
    qq = jnp.concatenate([jnp.where(sub < DIFF_QK, qt, zero), jnp.where(sub >= DIFF_QK, qt, zero)],
                         axis=1)
    strips = [slice(j * ATTN_STRIP, (j + 1) * ATTN_STRIP) for j in range(2 * tq // ATTN_STRIP)]
    q_strips = [qq[:, st] for st in strips]
    bound = bound_ref[...]

    def blocks(i, carry):
        l, acc = carry
        ls = [l[:, st] for st in strips]
        accs = [acc[:, st] for st in strips]
        kbs = [i * ATTN_BLOCKS + u for u in range(ATTN_BLOCKS)]
        s = [[_dot(k_ref[0, pl.ds(pl.multiple_of(kb * tk, tk), tk), :], q_st) for q_st in q_strips]
             for kb in kbs]
        for u, kb in enumerate(kbs):
            vblk = vt_ref[0, kb]
            for j in range(len(strips)):
                p = jnp.exp2(s[u][j] - bound)
                ls[j] = ls[j] + jnp.sum(p, axis=0, keepdims=True)
                accs[j] = accs[j] + _dot(vblk, p.astype(BF16))
        return jnp.concatenate(ls, axis=1), jnp.concatenate(accs, axis=1)

    assert nk % ATTN_BLOCKS == 0
    l, acc = lax.fori_loop(0, nk // ATTN_BLOCKS, blocks,
                           (jnp.zeros((1, 2 * tq), F32), jnp.zeros((LANES, 2 * tq), F32)))
    _attn_epilogue(acc, l, tq, g_ref, lam_ref, sub_ref, o_ref)


def _diff_attn(bounded, qt, kn, vt, proj, lambda_qk, subln, bound, tq=512):
    b, _, t = qt.shape
    nk, tk = vt.shape[1], vt.shape[3]
    assert nk >= 2 and nk % 2 == 0
    gcol = COL_GD * (1024 // LANES)
    in_specs = [
        pl.BlockSpec((1, LANES, tq), lambda bi, h, i: (bi, h, i)),
        pl.BlockSpec((1, t, LANES), lambda bi, h, i: (bi, 0, h)),
        pl.BlockSpec((1, nk, LANES, tk), lambda bi, h, i: (bi, 0, h, 0)),
        pl.BlockSpec((1, tq, LANES), lambda bi, h, i: (bi, i, gcol + h)),
        pl.BlockSpec((4, DIFF_QK), lambda bi, h, i: (0, 0)),
        pl.BlockSpec((1, LANES), lambda bi, h, i: (0, 0)),
    ]
    args = [qt, kn, vt, proj, lambda_qk, subln]
    if bounded:
        body = functools.partial(_diff_attn_bounded_kernel, nk)
        in_specs.append(pl.BlockSpec((1, 1), lambda bi, h, i: (0, 0)))
        args.append(bound)
        scratch = []
    else:
        body = functools.partial(_diff_attn_kernel, nk)
        scratch = [pltpu.VMEM((tk, 2 * tq), F32), pltpu.VMEM((tk, 2 * tq), F32),
                   pltpu.VMEM((tk, 2 * tq), BF16), pltpu.VMEM((tk, 2 * tq), BF16)]
    return pl.pallas_call(
        body,
        grid=(b, N_DIFF_HEADS, t // tq),
        in_specs=in_specs,
        out_specs=pl.BlockSpec((1, tq, LANES), lambda bi, h, i: (bi, i, h)),
        out_shape=jax.ShapeDtypeStruct((b, t, D_DIFF), BF16),
        scratch_shapes=scratch,
        compiler_params=pltpu.CompilerParams(
            dimension_semantics=("arbitrary", "arbitrary", "arbitrary"), vmem_limit_bytes=VMEM_LIMIT),
        name="diff_attn_bounded" if bounded else "diff_attn",
    )(*args)


def _out_proj_kernel(yr_ref, yd_ref, wr_ref, wd_ref, x_ref, o_ref):
    o_ref[...] = x_ref[...] + _dot(yr_ref[...], wr_ref[...]) + _dot(yd_ref[...], wd_ref[...])


def _out_proj(y_r, y_d, w_o, x2, tm=512, tn=1024):
    n = x2.shape[0]
    return pl.pallas_call(
        _out_proj_kernel,
        grid=(n // tm, D_MODEL // tn),
        in_specs=[
            pl.BlockSpec((tm, D_RWKV), lambda i, j: (i, 0)),
            pl.BlockSpec((tm, D_DIFF), lambda i, j: (i, 0)),
            pl.BlockSpec((D_RWKV, tn), lambda i, j: (0, j)),
            pl.BlockSpec((D_DIFF, tn), lambda i, j: (1, j)),
            pl.BlockSpec((tm, tn), lambda i, j: (i, j)),
        ],
        out_specs=pl.BlockSpec((tm, tn), lambda i, j: (i, j)),
        out_shape=jax.ShapeDtypeStruct((n, D_MODEL), F32),
        compiler_params=pltpu.CompilerParams(
            dimension_semantics=("arbitrary", "arbitrary"), vmem_limit_bytes=VMEM_LIMIT),
        name="out_proj",
    )(y_r, y_d, w_o, w_o, x2)


def _rope_tables(t):
    inv = ROPE_THETA ** (-jnp.arange(ROPE_HALF, dtype=F32) * 2.0 / ROPE_DIMS)
    ang = jnp.arange(t, dtype=F32)[:, None] * inv[None, :]
    lane = jnp.arange(LANES) % DIFF_QK
    freq = jnp.arange(ROPE_HALF)[:, None]
    lo = (lane[None, :] == freq).astype(F32)
    hi = (lane[None, :] == freq + ROPE_HALF).astype(F32)
    expand = functools.partial(jnp.dot, precision=lax.Precision.HIGHEST)
    cos, sin = jnp.cos(ang), jnp.sin(ang)
    c = expand(cos, lo + hi) + (lane >= ROPE_DIMS).astype(F32)[None, :]
    return c, expand(sin, hi), expand(-sin, lo)


def _prepare_weights(norm_gain, w_in, mu_shift, w0, w_up, a0, a_up, k_k, k_a, r_k, gn_gain, gn_bias,
                     q_norm_gain, k_norm_gain, subln_gain, w_out):
    d3 = 3 * D_RWKV
    shift_cols = d3 + 2 * LORA
    w_main = jnp.concatenate([w_in[:, :d3], w_in[:, shift_cols:]], axis=1).astype(BF16)
    w_lora = w_in[:, d3:shift_cols].astype(BF16)
    row = lambda a: a.reshape(1, -1).astype(F32)
    zpad = jnp.zeros((LORA, D_RWKV), F32)
    dirs = []
    for d in range(2):
        dirs.append((
            row(mu_shift[:d3]), row(0.5 * mu_shift[:d3]), row(mu_shift[d3:shift_cols]),
            row(0.5 * w0[d]), jnp.concatenate([0.5 * w_up[d], zpad], axis=0).astype(BF16),
            row(0.5 * a0[d]), jnp.concatenate([zpad, 0.5 * a_up[d]], axis=0).astype(BF16),
            row(k_k), row(k_a)))
    final_extra = (row(r_k), row(gn_gain), row(gn_bias))
    bound = (SCORE_BOUND_COEF * jnp.max(jnp.abs(q_norm_gain)) * jnp.max(jnp.abs(k_norm_gain))
             ).astype(F32).reshape(1, 1)
    attn = (row(jnp.tile(q_norm_gain, 2)), row(jnp.tile(k_norm_gain, 2)), row(subln_gain), bound)
    return row(norm_gain), w_main, w_lora, dirs, final_extra, attn, w_out.astype(BF16)


def _layer(x, weights, lambda_qk, rope):
    gain, w_main, w_lora, dirs, final_extra, attn, w_o = weights
    b, t, _ = x.shape
    x2 = x.reshape(b * t, D_MODEL)
    proj2, lora2 = _in_proj(x2, gain, w_main, w_lora)
    proj = proj2.reshape(b, t, D_MAIN)
    lora = lora2.reshape(b, t, 2 * LORA)

    o_f = _rwkv_call(False, False, proj, lora, dirs[0], None)
    y_r = _rwkv_call(True, True, proj, lora, dirs[1], final_extra + (o_f,))

    tk = 512
    cos, s1, s2 = rope
    q_gain, k_gain, subln, bound = attn
    qt, kn, vt = _attn_prep(proj, q_gain, k_gain, cos, s1, s2, tk)
    y_d = lax.cond(bound[0, 0] <= ATTN_BOUND_MAX,
                   functools.partial(_diff_attn, True), functools.partial(_diff_attn, False),
                   qt, kn, vt, proj, lambda_qk, subln, bound)

    out = _out_proj(y_r.reshape(b * t, D_RWKV), y_d.reshape(b * t, D_DIFF), w_o, x2)
    return out.reshape(b, t, D_MODEL)


def kernel(x_prompt, x_sample, norm_gain, w_in, mu_shift, w0, w_up, a0, a_up, k_k, k_a, r_k, gn_gain,
           gn_bias, q_norm_gain, k_norm_gain, lambda_qk, subln_gain, w_out):
    weights = _prepare_weights(norm_gain[0], w_in[0], mu_shift[0], w0[0], w_up[0], a0[0], a_up[0],
                               k_k[0], k_a[0], r_k[0], gn_gain[0], gn_bias[0], q_norm_gain[0],
                               k_norm_gain[0], subln_gain[0], w_out[0])
    lam = lambda_qk[0].astype(F32)
    rope = _rope_tables(max(x_prompt.shape[1], x_sample.shape[1]))
    return (_layer(x_prompt, weights, lam, rope), _layer(x_sample, weights, lam, rope))
```

```python
import functools
import math

import jax
import jax.numpy as jnp
from jax import lax
from jax.experimental import pallas as pl
from jax.experimental.pallas import tpu as pltpu

F32 = jnp.float32
BF16 = jnp.bfloat16

D_MODEL = 2048
D_RWKV = 1024
D_DIFF = 1024
RWKV_HEAD = 64
DIFF_VDIM = 128
DIFF_QK = 64
N_DIFF_HEADS = D_DIFF // DIFF_VDIM
LORA = 64
ROPE_DIMS = DIFF_QK // 4
ROPE_HALF = ROPE_DIMS // 2
ROPE_THETA = 500000.0
RMS_EPS = 1e-6
GN_EPS = 64e-5
DECAY_SCALE = 0.606531
LAMBDA_INIT = 0.8 - 0.6 * math.exp(-0.3 * 0)
LOG2E = 1.4426950408889634

LANES = 128
HALO = 8
BF16_ROWS = 16
CHUNK = 64
PAIRS = D_RWKV // LANES
D_MAIN = 8 * 1024
ATTN_UNROLL = 4
ATTN_STRIP = 256
ATTN_BLOCKS = 4
SCORE_BOUND_COEF = 1.01 * DIFF_QK * DIFF_QK ** -0.5 * LOG2E
ATTN_BOUND_MAX = 40.0
VMEM_LIMIT = 56 * 1024 * 1024

COL_R, COL_K, COL_V, COL_GR, COL_Q, COL_KD, COL_VD, COL_GD = range(8)


def _dot(a, b):
    return jnp.dot(a, b, preferred_element_type=F32)


def _dot_nt(a, b):
    return lax.dot_general(a, b, (((1,), (1,)), ((), ())), preferred_element_type=F32)


def _dot_tn(a, b):
    return lax.dot_general(a, b, (((0,), (0,)), ((), ())), preferred_element_type=F32)


def _split(x):
    hi = x.astype(BF16)
    return hi, (x - hi.astype(F32)).astype(BF16)


def _silu(x):
    h = 0.5 * x
    return h + h * jnp.tanh(h)


def _seg_ones(width):
    r = lax.broadcasted_iota(jnp.int32, (LANES, LANES), 0) // width
    c = lax.broadcasted_iota(jnp.int32, (LANES, LANES), 1) // width
    return (r == c).astype(BF16)


def _in_proj_kernel(x_ref, g_ref, w_ref, wl_ref, o_ref, ol_ref, h_scr):
    @pl.when(pl.program_id(1) == 0)
    def _():
        x = x_ref[...]
        ms = jnp.mean(x * x, axis=-1, keepdims=True)
        h = (x * lax.rsqrt(ms + RMS_EPS) * g_ref[...]).astype(BF16)
        h_scr[...] = h
        ol_ref[...] = _dot(h, wl_ref[...])

    o_ref[...] = _dot(h_scr[...], w_ref[...]).astype(o_ref.dtype)


def _in_proj(x2, gain, w_main, w_lora, tm=1024, tn=1024):
    n = x2.shape[0]
    return pl.pallas_call(
        _in_proj_kernel,
        grid=(n // tm, D_MAIN // tn),
        in_specs=[
            pl.BlockSpec((tm, D_MODEL), lambda i, j: (i, 0)),
            pl.BlockSpec((1, D_MODEL), lambda i, j: (0, 0)),
            pl.BlockSpec((D_MODEL, tn), lambda i, j: (0, j)),
            pl.BlockSpec((D_MODEL, 2 * LORA), lambda i, j: (0, 0)),
        ],
        out_specs=[
            pl.BlockSpec((tm, tn), lambda i, j: (i, j)),
            pl.BlockSpec((tm, 2 * LORA), lambda i, j: (i, 0)),
        ],
        out_shape=[
            jax.ShapeDtypeStruct((n, D_MAIN), BF16),
            jax.ShapeDtypeStruct((n, 2 * LORA), F32),
        ],
        scratch_shapes=[pltpu.VMEM((tm, D_MODEL), BF16)],
        compiler_params=pltpu.CompilerParams(
            dimension_semantics=("arbitrary", "arbitrary"), vmem_limit_bytes=VMEM_LIMIT),
        name="in_proj",
    )(x2, gain, w_main, w_lora)


def _shifted(x, prev_row, next_row, mu, row):
    tb = x.shape[0]
    prev = jnp.where(row == 0, prev_row, pltpu.roll(x, 1, 0))
    nxt = jnp.where(row == tb - 1, next_row, pltpu.roll(x, tb - 1, 0))
    return x + mu * (0.5 * (prev + nxt) - x)


def _rwkv_kernel(rev, final, tb, *refs):
    (r_ref, rp_ref, rn_ref, k_ref, kp_ref, kn_ref, v_ref, vp_ref, vn_ref,
     l_ref, lp_ref, ln_ref, mu_ref, half_ref, mul_ref,
     w0_ref, wup_ref, a0_ref, aup_ref, kk_ref, ka_ref) = refs[:21]
    if final:
        (rk_ref, gng_ref, gnb_ref, g_ref, of_ref, out_ref,
         a_s, r_s, bt_s, kt_s, bh_s, kh_s, v_s, dec_s, ob, bonus_s, state) = refs[21:]
    else:
        out_ref, a_s, r_s, bt_s, kt_s, bh_s, kh_s, v_s, dec_s, state = refs[21:]
        ob = out_ref.at[0]

    i = pl.program_id(1)
    nblk = pl.num_programs(1)
    blk = (nblk - 1 - i) if rev else i
    nchunk = tb // CHUNK

    @pl.when(i == 0)
    def _():
        state[...] = jnp.zeros_like(state)

    has_prev = (blk > 0).astype(F32)
    has_next = (blk < nblk - 1).astype(F32)
    ti = lax.broadcasted_iota(jnp.int32, (tb, tb), 0)
    tj = lax.broadcasted_iota(jnp.int32, (tb, tb), 1)
    delta = (0.5 * ((tj == ti - 1) | (tj == ti + 1)).astype(F32) - (tj == ti).astype(F32)).astype(BF16)
    row8 = lax.broadcasted_iota(jnp.int32, (HALO, 1), 0)
    row = lax.broadcasted_iota(jnp.int32, (tb, 1), 0)
    z = _shifted(l_ref[0], lp_ref[0, HALO - 1:HALO, :] * has_prev, ln_ref[0, 0:1, :] * has_next,
                 mul_ref[...], row)
    lane = lax.broadcasted_iota(jnp.int32, (tb, 2 * LORA), 1)
    zt = jnp.where(lane < LORA, jnp.tanh(z), z).astype(BF16)
    ones64 = _seg_ones(RWKV_HEAD)
    same = (ti // CHUNK) == (tj // CHUNK)
    order = (tj >= ti) if rev else (tj <= ti)
    tri_blk = (same & order).astype(BF16)
    pairs = range(PAIRS)
    lns = [slice(p * LANES, (p + 1) * LANES) for p in pairs]

    def seg(x):
        xb = x.astype(BF16)
        return jnp.concatenate([_dot(xb[:, ln], ones64) for ln in lns], axis=1)

    def shift(col, ref, p_ref, n_ref):
        cl = slice(col * D_RWKV, (col + 1) * D_RWKV)
        half_mu = half_ref[:, cl]
        xb = ref[0]
        y = xb.astype(F32) + _dot(delta, xb) * mu_ref[:, cl]
        top = jnp.where(row8 == 0,
                        p_ref[0, BF16_ROWS - 1:BF16_ROWS, :].astype(F32) * (half_mu * has_prev), 0.0)
        bot = jnp.where(row8 == HALO - 1, n_ref[0, 0:1, :].astype(F32) * (half_mu * has_next), 0.0)
        return jnp.concatenate([y[:HALO] + top, y[HALO:tb - HALO], y[tb - HALO:] + bot], axis=0)

    rr = shift(0, r_ref, rp_ref, rn_ref)
    kk_s = shift(1, k_ref, kp_ref, kn_ref)
    vv = shift(2, v_ref, vp_ref, vn_ref)
    logw = (-0.5 * DECAY_SCALE * LOG2E) * (1.0 + jnp.tanh(w0_ref[...] + _dot(zt, wup_ref[...])))
    asig = 0.5 + 0.5 * jnp.tanh(a0_ref[...] + _dot(zt, aup_ref[...]))
    kk = kk_s * kk_ref[...]
    kk = kk * lax.rsqrt(jnp.maximum(seg(kk * kk), 1e-12))
    kd = kk_s * (1.0 + (asig - 1.0) * ka_ref[...])
    bb = asig * kk
    if final:
        bonus_s[...] = seg(rr * kk_s * rk_ref[...]) * vv

    lw_hi, lw_lo = _split(logw)
    cum = _dot(tri_blk, lw_hi) + _dot(tri_blk, lw_lo)
    g_inv = jnp.exp2(-cum)
    bt = bb * g_inv
    kt = kd * g_inv
    a_s[...] = (-kk * jnp.exp2(cum - logw)).astype(BF16)
    r_s[...] = (rr * jnp.exp2(cum)).astype(BF16)
    bt_s[...] = bt.astype(BF16)
    kt_s[...] = kt.astype(BF16)
    v_s[...] = vv.astype(BF16)
    for c in range(nchunk):
        rows = slice(c * CHUNK, (c + 1) * CHUNK)
        end = c * CHUNK if rev else (c + 1) * CHUNK - 1
        dec = jnp.exp2(cum[end:end + 1, :])
        bh_s[rows, :] = (bt[rows] * dec).astype(BF16)
        kh_s[rows, :] = (kt[rows] * dec).astype(BF16)
        dec_s[c * HALO:(c + 1) * HALO, :] = jnp.broadcast_to(dec, (HALO, D_RWKV))

    t_i = lax.broadcasted_iota(jnp.int32, (CHUNK, LANES), 0)
    j_i = lax.broadcasted_iota(jnp.int32, (CHUNK, LANES), 1) % CHUNK
    if rev:
        strict, incl = j_i > t_i, j_i >= t_i
    else:
        strict, incl = j_i < t_i, j_i <= t_i
    eye_pair = (j_i == t_i).astype(F32)
    bd_mask = (lax.broadcasted_iota(jnp.int32, (LANES, LANES), 0) // CHUNK
               == lax.broadcasted_iota(jnp.int32, (LANES, LANES), 1) // CHUNK)
    zero_b = jnp.zeros((), BF16)

    def bd(xb):
        return jnp.where(bd_mask, jnp.concatenate([xb, xb], axis=0), zero_b)

    def stack(x, y):
        return jnp.concatenate([x, y], axis=0)

    items = [(c, p) for c in range(nchunk) for p in pairs]
    rows_of = lambda c: slice(c * CHUNK, (c + 1) * CHUNK)
    a_t = [a_s[rows_of(c), lns[p]] for c, p in items]
    r_t = [r_s[rows_of(c), lns[p]] for c, p in items]
    v_c = [v_s[rows_of(c), lns[p]] for c, p in items]
    bd_v = [bd(v) for v in v_c]
    n_items = range(len(items))

    q = [_dot_nt(stack(a_t[n], r_t[n]),
                 stack(bd(bt_s[rows_of(c), lns[p]]), bd(kt_s[rows_of(c), lns[p]])))
         for n, (c, p) in enumerate(items)]
    l_ab = [jnp.where(strict, x[:CHUNK, :LANES], 0.0) for x in q]
    l_ak = [jnp.where(strict, x[:CHUNK, LANES:], 0.0).astype(BF16) for x in q]
    m_r = [jnp.concatenate([jnp.where(incl, x[CHUNK:, :LANES], 0.0).astype(BF16),
                            jnp.where(incl, x[CHUNK:, LANES:], 0.0).astype(BF16)], axis=1) for x in q]
    lak_v = [_dot(l_ak[n], bd_v[n]) for n in n_items]

    t_inv = [eye_pair + x for x in l_ab]
    lk = [x.astype(BF16) for x in l_ab]
    lk = [_dot(x, bd(x)).astype(BF16) for x in lk]
    for _ in range(4):
        res = [_dot(stack(lk[n], t_inv[n].astype(BF16)), bd(lk[n])) for n in n_items]
        lk = [x[:CHUNK].astype(BF16) for x in res]
        t_inv = [t_inv[n] + res[n][CHUNK:] for n in n_items]
    t_inv = [(t_inv[n] + _dot(t_inv[n].astype(BF16), bd(lk[n]))).astype(BF16) for n in n_items]
    aw = [_dot(t_inv[n], jnp.concatenate([bd(a_t[n]), bd(lak_v[n].astype(BF16))], axis=1))
          for n in n_items]

    s = [state[p] for p in pairs]
    for c in (reversed(range(nchunk)) if rev else range(nchunk)):
        rows = rows_of(c)
        idx = [c * PAIRS + p for p in pairs]
        uo = [_dot_nt(stack(aw[idx[p]][:, :LANES].astype(BF16), r_t[idx[p]]), s[p].astype(BF16))
              for p in pairs]
        u = [(uo[p][:CHUNK] + aw[idx[p]][:, LANES:]).astype(BF16) for p in pairs]
        o = [uo[p][CHUNK:] + _dot(m_r[idx[p]], stack(bd(u[p]), bd_v[idx[p]])) for p in pairs]
        upd = [_dot_tn(stack(u[p], v_c[idx[p]]), stack(bh_s[rows, lns[p]], kh_s[rows, lns[p]]))
               for p in pairs]
        s = [s[p] * dec_s[c * HALO:c * HALO + 1, lns[p]] + jnp.where(bd_mask, upd[p], 0.0)
             for p in pairs]
        for p in pairs:
            ob[rows, lns[p]] = o[p]
    for p in pairs:
        state[p] = s[p]

    if final:
        o_all = ob[...] + of_ref[0]
        inv_n = 1.0 / RWKV_HEAD
        cen = o_all - seg(o_all) * inv_n
        var = seg(cen * cen) * inv_n
        on = cen * lax.rsqrt(var + GN_EPS) * gng_ref[...] + gnb_ref[...]
        out_ref[0] = ((on + bonus_s[...]) * _silu(g_ref[0].astype(F32))).astype(out_ref.dtype)


def _rwkv_call(rev, final, proj, lora, params, extra, tb=256):
    b, t, _ = proj.shape
    nblk = t // tb

    def blk(i):
        return (nblk - 1 - i) if rev else i

    def main_spec(col, width=1024):
        return pl.BlockSpec((1, tb, width), lambda bi, i: (bi, blk(i), col))

    def prev_spec(col, width, rows):
        return pl.BlockSpec((1, rows, width),
                            lambda bi, i: (bi, jnp.maximum(blk(i) * (tb // rows) - 1, 0), col))

    def next_spec(col, width, rows):
        return pl.BlockSpec((1, rows, width),
                            lambda bi, i: (bi, jnp.minimum((blk(i) + 1) * (tb // rows), t // rows - 1), col))

    def full_spec(a):
        return pl.BlockSpec(a.shape, lambda bi, i: (0,) * a.ndim)

    in_specs, args = [], []
    for col in (COL_R, COL_K, COL_V):
        in_specs += [main_spec(col), prev_spec(col, 1024, BF16_ROWS), next_spec(col, 1024, BF16_ROWS)]
        args += [proj, proj, proj]
    in_specs += [main_spec(0, 2 * LORA), prev_spec(0, 2 * LORA, HALO), next_spec(0, 2 * LORA, HALO)]
    args += [lora, lora, lora]
    for a in params:
        in_specs.append(full_spec(a))
        args.append(a)
    scratch = [pltpu.VMEM((tb, D_RWKV), BF16) for _ in range(7)]
    scratch.append(pltpu.VMEM((tb // CHUNK * HALO, D_RWKV), F32))
    if final:
        rk, gng, gnb, o_f = extra
        for a in (rk, gng, gnb):
            in_specs.append(full_spec(a))
            args.append(a)
        in_specs += [main_spec(COL_GR), pl.BlockSpec((1, tb, D_RWKV), lambda bi, i: (bi, blk(i), 0))]
        args += [proj, o_f]
        scratch += [pltpu.VMEM((tb, D_RWKV), F32), pltpu.VMEM((tb, D_RWKV), F32)]
        out_dtype = BF16
    else:
        out_dtype = F32
    scratch.append(pltpu.VMEM((PAIRS, LANES, LANES), F32))
    return pl.pallas_call(
        functools.partial(_rwkv_kernel, rev, final, tb),
        grid=(b, nblk),
        in_specs=in_specs,
        out_specs=pl.BlockSpec((1, tb, D_RWKV), lambda bi, i: (bi, blk(i), 0)),
        out_shape=jax.ShapeDtypeStruct((b, t, D_RWKV), out_dtype),
        scratch_shapes=scratch,
        compiler_params=pltpu.CompilerParams(
            dimension_semantics=("arbitrary", "arbitrary"), vmem_limit_bytes=VMEM_LIMIT),
        name="rwkv_bwd" if rev else "rwkv_fwd",
    )(*args)


def _attn_prep_kernel(q_ref, k_ref, v_ref, qg_ref, kg_ref, cos_ref, s1_ref, s2_ref,
                      qt_ref, ko_ref, vt_ref):
    ones64 = _seg_ones(DIFF_QK)
    cos, s1, s2 = cos_ref[...], s1_ref[...], s2_ref[...]

    def norm_rope(x, gain):
        ms = _dot((x * x).astype(BF16), ones64) * (1.0 / DIFF_QK)
        y = x * lax.rsqrt(ms + RMS_EPS) * gain
        return y * cos + pltpu.roll(y, ROPE_HALF, 1) * s1 + pltpu.roll(y, LANES - ROPE_HALF, 1) * s2

    for h in range(N_DIFF_HEADS):
        ln = slice(h * LANES, (h + 1) * LANES)
        qh = norm_rope(q_ref[0, :, ln].astype(F32), qg_ref[...]) * (DIFF_QK ** -0.5 * LOG2E)
        qt_ref[0, ln, :] = qh.T.astype(BF16)
        ko_ref[0, :, ln] = norm_rope(k_ref[0, :, ln].astype(F32), kg_ref[...]).astype(BF16)
        vt_ref[0, 0, ln, :] = v_ref[0, :, ln].astype(F32).T.astype(BF16)


def _attn_prep(proj, q_gain, k_gain, cos, s1, s2, tk):
    b, t, _ = proj.shape
    nk = t // tk
    tab = pl.BlockSpec((tk, LANES), lambda bi, i: (i, 0))
    gain = pl.BlockSpec((1, LANES), lambda bi, i: (0, 0))
    return pl.pallas_call(
        _attn_prep_kernel,
        grid=(b, nk),
        in_specs=[
            pl.BlockSpec((1, tk, D_DIFF), lambda bi, i: (bi, i, COL_Q)),
            pl.BlockSpec((1, tk, D_DIFF), lambda bi, i: (bi, i, COL_KD)),
            pl.BlockSpec((1, tk, D_DIFF), lambda bi, i: (bi, i, COL_VD)),
            gain, gain, tab, tab, tab,
        ],
        out_specs=[
            pl.BlockSpec((1, D_DIFF, tk), lambda bi, i: (bi, 0, i)),
            pl.BlockSpec((1, tk, D_DIFF), lambda bi, i: (bi, i, 0)),
            pl.BlockSpec((1, 1, D_DIFF, tk), lambda bi, i: (bi, i, 0, 0)),
        ],
        out_shape=[
            jax.ShapeDtypeStruct((b, D_DIFF, t), BF16),
            jax.ShapeDtypeStruct((b, t, D_DIFF), BF16),
            jax.ShapeDtypeStruct((b, nk, D_DIFF, tk), BF16),
        ],
        compiler_params=pltpu.CompilerParams(
            dimension_semantics=("arbitrary", "arbitrary"), vmem_limit_bytes=VMEM_LIMIT),
        name="attn_prep",
    )(proj, proj, proj, q_gain, k_gain, cos, s1, s2)


def _attn_epilogue(acc, l, tq, g_ref, lam_ref, sub_ref, o_ref):
    on = acc * (1.0 / l)
    lq = lam_ref[...]
    lam = (jnp.exp(jnp.sum(lq[0:1] * lq[1:2], axis=-1, keepdims=True))
           - jnp.exp(jnp.sum(lq[2:3] * lq[3:4], axis=-1, keepdims=True)) + LAMBDA_INIT)
    o = (on[:, :tq] - lam * on[:, tq:]).T
    ms = jnp.mean(o * o, axis=-1, keepdims=True)
    y = o * lax.rsqrt(ms + RMS_EPS) * sub_ref[...] * (1.0 - LAMBDA_INIT)
    o_ref[0] = (y * _silu(g_ref[0].astype(F32))).astype(o_ref.dtype)


def _diff_attn_kernel(nk, qt_ref, k_ref, vt_ref, g_ref, lam_ref, sub_ref, o_ref,
                      sa_ref, sb_ref, pa_ref, pb_ref):
    qt = qt_ref[0]
    tq = qt.shape[1]
    tk = vt_ref.shape[3]
    sub = lax.broadcasted_iota(jnp.int32, (LANES, tq), 0)
    zero = jnp.zeros((), BF16)
    qq = jnp.concatenate([jnp.where(sub < DIFF_QK, qt, zero), jnp.where(sub >= DIFF_QK, qt, zero)],
                         axis=1)

    def scores(kb, s_ref):
        kblk = k_ref[0, pl.ds(pl.multiple_of(kb * tk, tk), tk), :]
        s_ref[...] = _dot(kblk, qq)

    def softmax_step(s_ref, p_ref, m, l):
        ms, ls, als = [], [], []
        row_chunks = [slice(r * LANES, (r + 1) * LANES) for r in range(tk // LANES)]
        for j in range(2 * tq // LANES):
            ln = slice(j * LANES, (j + 1) * LANES)
            m_new = m[:, ln]
            for rc in row_chunks:
                m_new = jnp.maximum(m_new, jnp.max(s_ref[rc, ln], axis=0, keepdims=True))
            alpha = jnp.exp2(m[:, ln] - m_new)
            l_new = alpha * l[:, ln]
            for rc in row_chunks:
                p = jnp.exp2(s_ref[rc, ln] - m_new)
                p_ref[rc, ln] = p.astype(BF16)
                l_new = l_new + jnp.sum(p, axis=0, keepdims=True)
            ms.append(m_new)
            als.append(alpha)
            ls.append(l_new)
        cat = lambda xs: jnp.concatenate(xs, axis=1)
        return cat(als), cat(ms), cat(ls)

    m = jnp.full((1, 2 * tq), -jnp.inf, F32)
    l = jnp.zeros((1, 2 * tq), F32)
    acc = jnp.zeros((LANES, 2 * tq), F32)
    scores(0, sa_ref)
    al_a, m, l = softmax_step(sa_ref, pa_ref, m, l)
    scores(1, sb_ref)

    def stage(kb, acc, alpha, m, l, s_mine, p_mine, s_other, p_other, last):
        if not last:
            scores(kb + 2, s_mine)
        pv = _dot(vt_ref[0, kb], p_mine[...])
        alpha_other, m, l = softmax_step(s_other, p_other, m, l)
        return alpha * acc + pv, alpha_other, m, l

    bufs = ((sa_ref, pa_ref), (sb_ref, pb_ref))

    def run_stages(kb0, count, carry, tail):
        m, l, acc, alpha = carry
        for u in range(count):
            (s_mine, p_mine), (s_other, p_other) = bufs[u % 2], bufs[(u + 1) % 2]
            acc, alpha, m, l = stage(kb0 + u, acc, alpha, m, l, s_mine, p_mine, s_other, p_other,
                                     tail and u >= count - 1)
        return m, l, acc, alpha

    n_loop = (nk - 2) // ATTN_UNROLL
    carry = lax.fori_loop(0, n_loop, lambda i, c: run_stages(ATTN_UNROLL * i, ATTN_UNROLL, c, False),
                          (m, l, acc, al_a))
    done = n_loop * ATTN_UNROLL
    m, l, acc, alpha = run_stages(done, nk - 1 - done, carry, True)
    acc = alpha * acc + _dot(vt_ref[0, nk - 1], bufs[(nk - 1) % 2][1][...])
    _attn_epilogue(acc, l, tq, g_ref, lam_ref, sub_ref, o_ref)


def _diff_attn_bounded_kernel(nk, qt_ref, k_ref, vt_ref, g_ref, lam_ref, sub_ref, bound_ref, o_ref):
    qt = qt_ref[0]
    tq = qt.shape[1]
    tk = vt_ref.shape[3]
    sub = lax.broadcasted_iota(jnp.int32, (LANES, tq), 0)
    zero = jnp.zeros((), BF16)
    qq = jnp.concatenate([jnp.where(sub < DIFF_QK, qt, zero), jnp.where(sub >= DIFF_QK, qt, zero)],
                         axis=1)
    strips = [slice(j * ATTN_STRIP, (j + 1) * ATTN_STRIP) for j in range(2 * tq // ATTN_STRIP)]
    q_strips = [qq[:, st] for st in strips]
    bound = bound_ref[...]

    def blocks(i, carry):
        l, acc = carry
        ls = [l[:, st] for st in strips]
        accs = [acc[:, st] for st in strips]
        kbs = [i * ATTN_BLOCKS + u for u in range(ATTN_BLOCKS)]
        s = [[_dot(k_ref[0, pl.ds(pl.multiple_of(kb * tk, tk), tk), :], q_st) for q_st in q_strips]
             for kb in kbs]
        for u, kb in enumerate(kbs):
            vblk = vt_ref[0, kb]
            for j in range(len(strips)):
                p = jnp.exp2(s[u][j] - bound)
                ls[j] = ls[j] + jnp.sum(p, axis=0, keepdims=True)
                accs[j] = accs[j] + _dot(vblk, p.astype(BF16))
        return jnp.concatenate(ls, axis=1), jnp.concatenate(accs, axis=1)

    assert nk % ATTN_BLOCKS == 0
    l, acc = lax.fori_loop(0, nk // ATTN_BLOCKS, blocks,
                           (jnp.zeros((1, 2 * tq), F32), jnp.zeros((LANES, 2 * tq), F32)))
    _attn_epilogue(acc, l, tq, g_ref, lam_ref, sub_ref, o_ref)


def _diff_attn(bounded, qt, kn, vt, proj, lambda_qk, subln, bound, tq=512):
    b, _, t = qt.shape
    nk, tk = vt.shape[1], vt.shape[3]
    assert nk >= 2 and nk % 2 == 0
    gcol = COL_GD * (1024 // LANES)
    in_specs = [
        pl.BlockSpec((1, LANES, tq), lambda bi, h, i: (bi, h, i)),
        pl.BlockSpec((1, t, LANES), lambda bi, h, i: (bi, 0, h)),
        pl.BlockSpec((1, nk, LANES, tk), lambda bi, h, i: (bi, 0, h, 0)),
        pl.BlockSpec((1, tq, LANES), lambda bi, h, i: (bi, i, gcol + h)),
        pl.BlockSpec((4, DIFF_QK), lambda bi, h, i: (0, 0)),
        pl.BlockSpec((1, LANES), lambda bi, h, i: (0, 0)),
    ]
    args = [qt, kn, vt, proj, lambda_qk, subln]
    if bounded:
        body = functools.partial(_diff_attn_bounded_kernel, nk)
        in_specs.append(pl.BlockSpec((1, 1), lambda bi, h, i: (0, 0)))
        args.append(bound)
        scratch = []
    else:
        body = functools.partial(_diff_attn_kernel, nk)
        scratch = [pltpu.VMEM((tk, 2 * tq), F32), pltpu.VMEM((tk, 2 * tq), F32),
                   pltpu.VMEM((tk, 2 * tq), BF16), pltpu.VMEM((tk, 2 * tq), BF16)]
    return pl.pallas_call(
        body,
        grid=(b, N_DIFF_HEADS, t // tq),
        in_specs=in_specs,
        out_specs=pl.BlockSpec((1, tq, LANES), lambda bi, h, i: (bi, i, h)),
        out_shape=jax.ShapeDtypeStruct((b, t, D_DIFF), BF16),
        scratch_shapes=scratch,
        compiler_params=pltpu.CompilerParams(
            dimension_semantics=("arbitrary", "arbitrary", "arbitrary"), vmem_limit_bytes=VMEM_LIMIT),
        name="diff_attn_bounded" if bounded else "diff_attn",
    )(*args)


def _out_proj_kernel(yr_ref, yd_ref, wr_ref, wd_ref, x_ref, o_ref):
    o_ref[...] = x_ref[...] + _dot(yr_ref[...], wr_ref[...]) + _dot(yd_ref[...], wd_ref[...])


def _out_proj(y_r, y_d, w_o, x2, tm=512, tn=D_MODEL):
    n = x2.shape[0]
    return pl.pallas_call(
        _out_proj_kernel,
        grid=(n // tm, D_MODEL // tn),
        in_specs=[
            pl.BlockSpec((tm, D_RWKV), lambda i, j: (i, 0)),
            pl.BlockSpec((tm, D_DIFF), lambda i, j: (i, 0)),
            pl.BlockSpec((D_RWKV, tn), lambda i, j: (0, j)),
            pl.BlockSpec((D_DIFF, tn), lambda i, j: (1, j)),
            pl.BlockSpec((tm, tn), lambda i, j: (i, j)),
        ],
        out_specs=pl.BlockSpec((tm, tn), lambda i, j: (i, j)),
        out_shape=jax.ShapeDtypeStruct((n, D_MODEL), F32),
        compiler_params=pltpu.CompilerParams(
            dimension_semantics=("arbitrary", "arbitrary"), vmem_limit_bytes=VMEM_LIMIT),
        name="out_proj",
    )(y_r, y_d, w_o, w_o, x2)


def _rope_tables(t):
    inv = ROPE_THETA ** (-jnp.arange(ROPE_HALF, dtype=F32) * 2.0 / ROPE_DIMS)
    ang = jnp.arange(t, dtype=F32)[:, None] * inv[None, :]
    lane = jnp.arange(LANES) % DIFF_QK
    freq = jnp.arange(ROPE_HALF)[:, None]
    lo = (lane[None, :] == freq).astype(F32)
    hi = (lane[None, :] == freq + ROPE_HALF).astype(F32)
    expand = functools.partial(jnp.dot, precision=lax.Precision.HIGHEST)
    cos, sin = jnp.cos(ang), jnp.sin(ang)
    c = expand(cos, lo + hi) + (lane >= ROPE_DIMS).astype(F32)[None, :]
    return c, expand(sin, hi), expand(-sin, lo)


def _prepare_weights(norm_gain, w_in, mu_shift, w0, w_up, a0, a_up, k_k, k_a, r_k, gn_gain, gn_bias,
                     q_norm_gain, k_norm_gain, subln_gain, w_out):
    d3 = 3 * D_RWKV
    shift_cols = d3 + 2 * LORA
    w_main = jnp.concatenate([w_in[:, :d3], w_in[:, shift_cols:]], axis=1).astype(BF16)
    w_lora = w_in[:, d3:shift_cols].astype(BF16)
    row = lambda a: a.reshape(1, -1).astype(F32)
    zpad = jnp.zeros((LORA, D_RWKV), F32)
    dirs = []
    for d in range(2):
        dirs.append((
            row(mu_shift[:d3]), row(0.5 * mu_shift[:d3]), row(mu_shift[d3:shift_cols]),
            row(0.5 * w0[d]), jnp.concatenate([0.5 * w_up[d], zpad], axis=0).astype(BF16),
            row(0.5 * a0[d]), jnp.concatenate([zpad, 0.5 * a_up[d]], axis=0).astype(BF16),
            row(k_k), row(k_a)))
    final_extra = (row(r_k), row(gn_gain), row(gn_bias))
    bound = (SCORE_BOUND_COEF * jnp.max(jnp.abs(q_norm_gain)) * jnp.max(jnp.abs(k_norm_gain))
             ).astype(F32).reshape(1, 1)
    attn = (row(jnp.tile(q_norm_gain, 2)), row(jnp.tile(k_norm_gain, 2)), row(subln_gain), bound)
    return row(norm_gain), w_main, w_lora, dirs, final_extra, attn, w_out.astype(BF16)


def _layer(x, weights, lambda_qk, rope):
    gain, w_main, w_lora, dirs, final_extra, attn, w_o = weights
    b, t, _ = x.shape
    x2 = x.reshape(b * t, D_MODEL)
    proj2, lora2 = _in_proj(x2, gain, w_main, w_lora)
    proj = proj2.reshape(b, t, D_MAIN)
    lora = lora2.reshape(b, t, 2 * LORA)

    o_f = _rwkv_call(False, False, proj, lora, dirs[0], None)
    y_r = _rwkv_call(True, True, proj, lora, dirs[1], final_extra + (o_f,))

    tk = 512
    cos, s1, s2 = rope
    q_gain, k_gain, subln, bound = attn
    qt, kn, vt = _attn_prep(proj, q_gain, k_gain, cos, s1, s2, tk)
    y_d = lax.cond(bound[0, 0] <= ATTN_BOUND_MAX,
                   functools.partial(_diff_attn, True), functools.partial(_diff_attn, False),
                   qt, kn, vt, proj, lambda_qk, subln, bound)

    out = _out_proj(y_r.reshape(b * t, D_RWKV), y_d.reshape(b * t, D_DIFF), w_o, x2)
    return out.reshape(b, t, D_MODEL)


def kernel(x_prompt, x_sample, norm_gain, w_in, mu_shift, w0, w_up, a0, a_up, k_k, k_a, r_k, gn_gain,
           gn_bias, q_norm_gain, k_norm_gain, lambda_qk, subln_gain, w_out):
    weights = _prepare_weights(norm_gain[0], w_in[0], mu_shift[0], w0[0], w_up[0], a0[0], a_up[0],
                               k_k[0], k_a[0], r_k[0], gn_gain[0], gn_bias[0], q_norm_gain[0],
                               k_norm_gain[0], subln_gain[0], w_out[0])
    lam = lambda_qk[0].astype(F32)
    rope = _rope_tables(max(x_prompt.shape[1], x_sample.shape[1]))
    return (_layer(x_prompt, weights, lam, rope), _layer(x_sample, weights, lam, rope))
```

```python
import functools
import math

import jax
import jax.numpy as jnp
from jax import lax
from jax.experimental import pallas as pl
from jax.experimental.pallas import tpu as pltpu

F32 = jnp.float32
BF16 = jnp.bfloat16

D_MODEL = 2048
D_RWKV = 1024
D_DIFF = 1024
RWKV_HEAD = 64
DIFF_VDIM = 128
DIFF_QK = 64
N_DIFF_HEADS = D_DIFF // DIFF_VDIM
LORA = 64
ROPE_DIMS = DIFF_QK // 4
ROPE_HALF = ROPE_DIMS // 2
ROPE_THETA = 500000.0
RMS_EPS = 1e-6
GN_EPS = 64e-5
DECAY_SCALE = 0.606531
LAMBDA_INIT = 0.8 - 0.6 * math.exp(-0.3 * 0)
LOG2E = 1.4426950408889634

LANES = 128
HALO = 8
CHUNK = 64
PAIRS = D_RWKV // LANES
D_MAIN = 8 * 1024
SHIFT_TILES = 3
ATTN_UNROLL = 4
ATTN_STRIP = 256
ATTN_BLOCKS = 4
SCORE_BOUND_COEF = 1.01 * DIFF_QK * DIFF_QK ** -0.5 * LOG2E
ATTN_BOUND_MAX = 40.0
VMEM_LIMIT = 56 * 1024 * 1024

COL_R, COL_K, COL_V, COL_GR, COL_Q, COL_KD, COL_VD, COL_GD = range(8)


def _dot(a, b):
    return jnp.dot(a, b, preferred_element_type=F32)


def _dot_nt(a, b):
    return lax.dot_general(a, b, (((1,), (1,)), ((), ())), preferred_element_type=F32)


def _dot_tn(a, b):
    return lax.dot_general(a, b, (((0,), (0,)), ((), ())), preferred_element_type=F32)


def _split(x):
    hi = x.astype(BF16)
    return hi, (x - hi.astype(F32)).astype(BF16)


def _silu(x):
    h = 0.5 * x
    return h + h * jnp.tanh(h)


def _seg_ones(width):
    r = lax.broadcasted_iota(jnp.int32, (LANES, LANES), 0) // width
    c = lax.broadcasted_iota(jnp.int32, (LANES, LANES), 1) // width
    return (r == c).astype(BF16)


def _shifted(x, prev_row, next_row, mu):
    tm = x.shape[0]
    both = pltpu.roll(x, 1, 0) + pltpu.roll(x, tm - 1, 0)
    row8 = lax.broadcasted_iota(jnp.int32, (HALO, 1), 0)
    top = both[:HALO] + jnp.where(row8 == 0, prev_row - x[tm - 1:tm], 0.0)
    bot = both[tm - HALO:] + jnp.where(row8 == HALO - 1, next_row - x[0:1], 0.0)
    both = jnp.concatenate([top, both[HALO:tm - HALO], bot], axis=0)
    return x * (1.0 - mu) + both * (0.5 * mu)


def _in_proj_kernel(blocks_per_seq, x_ref, xp_ref, xn_ref, g_ref, w_ref, wl_ref, mu_ref, mul_ref,
                    o_ref, ol_ref, h_scr, hh_scr):
    i, j = pl.program_id(0), pl.program_id(1)
    pos = i % blocks_per_seq
    has_prev = (pos > 0).astype(F32)
    has_next = (pos < blocks_per_seq - 1).astype(F32)

    def norm(x):
        ms = jnp.mean(x * x, axis=-1, keepdims=True)
        return (x * lax.rsqrt(ms + RMS_EPS) * g_ref[...]).astype(BF16)

    def shifted_product(h, hh, w, mu):
        edge = _dot(hh, w)
        return _shifted(_dot(h, w), edge[HALO - 1:HALO] * has_prev, edge[HALO:HALO + 1] * has_next, mu)

    @pl.when(j == 0)
    def _():
        h = norm(x_ref[...])
        hh = norm(jnp.concatenate([xp_ref[...], xn_ref[...]], axis=0))
        h_scr[...] = h
        hh_scr[...] = hh
        ol_ref[...] = shifted_product(h, hh, wl_ref[...], mul_ref[...])

    @pl.when(j < SHIFT_TILES)
    def _():
        o_ref[...] = shifted_product(h_scr[...], hh_scr[...], w_ref[...], mu_ref[...]).astype(o_ref.dtype)

    @pl.when(j >= SHIFT_TILES)
    def _():
        o_ref[...] = _dot(h_scr[...], w_ref[...]).astype(o_ref.dtype)


def _in_proj(x2, seq_len, gain, w_main, w_lora, mu_main, mu_lora, tm=1024, tn=1024):
    n = x2.shape[0]
    nh = n // HALO
    return pl.pallas_call(
        functools.partial(_in_proj_kernel, seq_len // tm),
        grid=(n // tm, D_MAIN // tn),
        in_specs=[
            pl.BlockSpec((tm, D_MODEL), lambda i, j: (i, 0)),
            pl.BlockSpec((HALO, D_MODEL), lambda i, j: (jnp.maximum(i * (tm // HALO) - 1, 0), 0)),
            pl.BlockSpec((HALO, D_MODEL), lambda i, j: (jnp.minimum((i + 1) * (tm // HALO), nh - 1), 0)),
            pl.BlockSpec((1, D_MODEL), lambda i, j: (0, 0)),
            pl.BlockSpec((D_MODEL, tn), lambda i, j: (0, j)),
            pl.BlockSpec((D_MODEL, 2 * LORA), lambda i, j: (0, 0)),
            pl.BlockSpec((1, tn), lambda i, j: (0, jnp.minimum(j, SHIFT_TILES - 1))),
            pl.BlockSpec((1, 2 * LORA), lambda i, j: (0, 0)),
        ],
        out_specs=[
            pl.BlockSpec((tm, tn), lambda i, j: (i, j)),
            pl.BlockSpec((tm, 2 * LORA), lambda i, j: (i, 0)),
        ],
        out_shape=[
            jax.ShapeDtypeStruct((n, D_MAIN), BF16),
            jax.ShapeDtypeStruct((n, 2 * LORA), F32),
        ],
        scratch_shapes=[pltpu.VMEM((tm, D_MODEL), BF16), pltpu.VMEM((2 * HALO, D_MODEL), BF16)],
        compiler_params=pltpu.CompilerParams(
            dimension_semantics=("arbitrary", "arbitrary"), vmem_limit_bytes=VMEM_LIMIT),
        name="in_proj",
    )(x2, x2, x2, gain, w_main, w_lora, mu_main, mu_lora)


def _rwkv_kernel(rev, final, tb, *refs):
    r_ref, k_ref, v_ref, l_ref, w0_ref, wup_ref, a0_ref, aup_ref, kk_ref, ka_ref = refs[:10]
    if final:
        (rk_ref, gng_ref, gnb_ref, g_ref, of_ref, out_ref,
         a_s, r_s, bt_s, kt_s, bh_s, kh_s, v_s, dec_s, ob, bonus_s, state) = refs[10:]
    else:
        out_ref, a_s, r_s, bt_s, kt_s, bh_s, kh_s, v_s, dec_s, state = refs[10:]
        ob = out_ref.at[0]

    nchunk = tb // CHUNK

    @pl.when(pl.program_id(1) == 0)
    def _():
        state[...] = jnp.zeros_like(state)

    ti = lax.broadcasted_iota(jnp.int32, (tb, tb), 0)
    tj = lax.broadcasted_iota(jnp.int32, (tb, tb), 1)
    z = l_ref[0]
    lane = lax.broadcasted_iota(jnp.int32, (tb, 2 * LORA), 1)
    zt = jnp.where(lane < LORA, jnp.tanh(z), z).astype(BF16)
    ones64 = _seg_ones(RWKV_HEAD)
    same = (ti // CHUNK) == (tj // CHUNK)
    order = (tj >= ti) if rev else (tj <= ti)
    tri_blk = (same & order).astype(BF16)
    pairs = range(PAIRS)
    lns = [slice(p * LANES, (p + 1) * LANES) for p in pairs]

    def seg(x):
        xb = x.astype(BF16)
        return jnp.concatenate([_dot(xb[:, ln], ones64) for ln in lns], axis=1)

    rr = r_ref[0].astype(F32)
    kk_s = k_ref[0].astype(F32)
    vv = v_ref[0].astype(F32)
    logw = (-0.5 * DECAY_SCALE * LOG2E) * (1.0 + jnp.tanh(w0_ref[...] + _dot(zt, wup_ref[...])))
    asig = 0.5 + 0.5 * jnp.tanh(a0_ref[...] + _dot(zt, aup_ref[...]))
    kk = kk_s * kk_ref[...]
    kk = kk * lax.rsqrt(jnp.maximum(seg(kk * kk), 1e-12))
    kd = kk_s * (1.0 + (asig - 1.0) * ka_ref[...])
    bb = asig * kk
    if final:
        bonus_s[...] = seg(rr * kk_s * rk_ref[...]) * vv

    lw_hi, lw_lo = _split(logw)
    cum = _dot(tri_blk, lw_hi) + _dot(tri_blk, lw_lo)
    g_inv = jnp.exp2(-cum)
    bt = bb * g_inv
    kt = kd * g_inv
    a_s[...] = (-kk * jnp.exp2(cum - logw)).astype(BF16)
    r_s[...] = (rr * jnp.exp2(cum)).astype(BF16)
    bt_s[...] = bt.astype(BF16)
    kt_s[...] = kt.astype(BF16)
    v_s[...] = vv.astype(BF16)
    for c in range(nchunk):
        rows = slice(c * CHUNK, (c + 1) * CHUNK)
        end = c * CHUNK if rev else (c + 1) * CHUNK - 1
        dec = jnp.exp2(cum[end:end + 1, :])
        bh_s[rows, :] = (bt[rows] * dec).astype(BF16)
        kh_s[rows, :] = (kt[rows] * dec).astype(BF16)
        dec_s[c * HALO:(c + 1) * HALO, :] = jnp.broadcast_to(dec, (HALO, D_RWKV))

    t_i = lax.broadcasted_iota(jnp.int32, (CHUNK, LANES), 0)
    j_i = lax.broadcasted_iota(jnp.int32, (CHUNK, LANES), 1) % CHUNK
    if rev:
        strict, incl = j_i > t_i, j_i >= t_i
    else:
        strict, incl = j_i < t_i, j_i <= t_i
    eye_pair = (j_i == t_i).astype(F32)
    bd_mask = (lax.broadcasted_iota(jnp.int32, (LANES, LANES), 0) // CHUNK
               == lax.broadcasted_iota(jnp.int32, (LANES, LANES), 1) // CHUNK)
    zero_b = jnp.zeros((), BF16)

    def bd(xb):
        return jnp.where(bd_mask, jnp.concatenate([xb, xb], axis=0), zero_b)

    def stack(x, y):
        return jnp.concatenate([x, y], axis=0)

    items = [(c, p) for c in range(nchunk) for p in pairs]
    rows_of = lambda c: slice(c * CHUNK, (c + 1) * CHUNK)
    a_t = [a_s[rows_of(c), lns[p]] for c, p in items]
    r_t = [r_s[rows_of(c), lns[p]] for c, p in items]
    v_c = [v_s[rows_of(c), lns[p]] for c, p in items]
    bd_v = [bd(v) for v in v_c]
    n_items = range(len(items))

    q = [_dot_nt(stack(a_t[n], r_t[n]),
                 stack(bd(bt_s[rows_of(c), lns[p]]), bd(kt_s[rows_of(c), lns[p]])))
         for n, (c, p) in enumerate(items)]
    l_ab = [jnp.where(strict, x[:CHUNK, :LANES], 0.0) for x in q]
    l_ak = [jnp.where(strict, x[:CHUNK, LANES:], 0.0).astype(BF16) for x in q]
    m_r = [jnp.concatenate([jnp.where(incl, x[CHUNK:, :LANES], 0.0).astype(BF16),
                            jnp.where(incl, x[CHUNK:, LANES:], 0.0).astype(BF16)], axis=1) for x in q]
    lak_v = [_dot(l_ak[n], bd_v[n]) for n in n_items]

    t_inv = [eye_pair + x for x in l_ab]
    lk = [x.astype(BF16) for x in l_ab]
    lk = [_dot(x, bd(x)).astype(BF16) for x in lk]
    for _ in range(4):
        res = [_dot(stack(lk[n], t_inv[n].astype(BF16)), bd(lk[n])) for n in n_items]
        lk = [x[:CHUNK].astype(BF16) for x in res]
        t_inv = [t_inv[n] + res[n][CHUNK:] for n in n_items]
    t_inv = [(t_inv[n] + _dot(t_inv[n].astype(BF16), bd(lk[n]))).astype(BF16) for n in n_items]
    aw = [_dot(t_inv[n], jnp.concatenate([bd(a_t[n]), bd(lak_v[n].astype(BF16))], axis=1))
          for n in n_items]

    s = [state[p] for p in pairs]
    for c in (reversed(range(nchunk)) if rev else range(nchunk)):
        rows = rows_of(c)
        idx = [c * PAIRS + p for p in pairs]
        uo = [_dot_nt(stack(aw[idx[p]][:, :LANES].astype(BF16), r_t[idx[p]]), s[p].astype(BF16))
              for p in pairs]
        u = [(uo[p][:CHUNK] + aw[idx[p]][:, LANES:]).astype(BF16) for p in pairs]
        o = [uo[p][CHUNK:] + _dot(m_r[idx[p]], stack(bd(u[p]), bd_v[idx[p]])) for p in pairs]
        upd = [_dot_tn(stack(u[p], v_c[idx[p]]), stack(bh_s[rows, lns[p]], kh_s[rows, lns[p]]))
               for p in pairs]
        s = [s[p] * dec_s[c * HALO:c * HALO + 1, lns[p]] + jnp.where(bd_mask, upd[p], 0.0)
             for p in pairs]
        for p in pairs:
            ob[rows, lns[p]] = o[p]
    for p in pairs:
        state[p] = s[p]

    if final:
        o_all = ob[...] + of_ref[0]
        inv_n = 1.0 / RWKV_HEAD
        cen = o_all - seg(o_all) * inv_n
        var = seg(cen * cen) * inv_n
        on = cen * lax.rsqrt(var + GN_EPS) * gng_ref[...] + gnb_ref[...]
        out_ref[0] = ((on + bonus_s[...]) * _silu(g_ref[0].astype(F32))).astype(out_ref.dtype)


def _rwkv_call(rev, final, proj, lora, params, extra, tb=256):
    b, t, _ = proj.shape
    nblk = t // tb

    def blk(i):
        return (nblk - 1 - i) if rev else i

    def main_spec(col, width=1024):
        return pl.BlockSpec((1, tb, width), lambda bi, i: (bi, blk(i), col))

    def full_spec(a):
        return pl.BlockSpec(a.shape, lambda bi, i: (0,) * a.ndim)

    in_specs = [main_spec(COL_R), main_spec(COL_K), main_spec(COL_V), main_spec(0, 2 * LORA)]
    args = [proj, proj, proj, lora]
    for a in params:
        in_specs.append(full_spec(a))
        args.append(a)
    scratch = [pltpu.VMEM((tb, D_RWKV), BF16) for _ in range(7)]
    scratch.append(pltpu.VMEM((tb // CHUNK * HALO, D_RWKV), F32))
    if final:
        rk, gng, gnb, o_f = extra
        for a in (rk, gng, gnb):
            in_specs.append(full_spec(a))
            args.append(a)
        in_specs += [main_spec(COL_GR), pl.BlockSpec((1, tb, D_RWKV), lambda bi, i: (bi, blk(i), 0))]
        args += [proj, o_f]
        scratch += [pltpu.VMEM((tb, D_RWKV), F32), pltpu.VMEM((tb, D_RWKV), F32)]
        out_dtype = BF16
    else:
        out_dtype = F32
    scratch.append(pltpu.VMEM((PAIRS, LANES, LANES), F32))
    return pl.pallas_call(
        functools.partial(_rwkv_kernel, rev, final, tb),
        grid=(b, nblk),
        in_specs=in_specs,
        out_specs=pl.BlockSpec((1, tb, D_RWKV), lambda bi, i: (bi, blk(i), 0)),
        out_shape=jax.ShapeDtypeStruct((b, t, D_RWKV), out_dtype),
        scratch_shapes=scratch,
        compiler_params=pltpu.CompilerParams(
            dimension_semantics=("arbitrary", "arbitrary"), vmem_limit_bytes=VMEM_LIMIT),
        name="rwkv_bwd" if rev else "rwkv_fwd",
    )(*args)


def _attn_prep_kernel(q_ref, k_ref, v_ref, qg_ref, kg_ref, cos_ref, s1_ref, s2_ref,
                      qt_ref, ko_ref, vt_ref):
    ones64 = _seg_ones(DIFF_QK)
    cos, s1, s2 = cos_ref[...], s1_ref[...], s2_ref[...]

    def norm_rope(x, gain):
        ms = _dot((x * x).astype(BF16), ones64) * (1.0 / DIFF_QK)
        y = x * lax.rsqrt(ms + RMS_EPS) * gain
        return y * cos + pltpu.roll(y, ROPE_HALF, 1) * s1 + pltpu.roll(y, LANES - ROPE_HALF, 1) * s2

    for h in range(N_DIFF_HEADS):
        ln = slice(h * LANES, (h + 1) * LANES)
        qh = norm_rope(q_ref[0, :, ln].astype(F32), qg_ref[...]) * (DIFF_QK ** -0.5 * LOG2E)
        qt_ref[0, ln, :] = qh.T.astype(BF16)
        ko_ref[0, :, ln] = norm_rope(k_ref[0, :, ln].astype(F32), kg_ref[...]).astype(BF16)
        vt_ref[0, 0, ln, :] = v_ref[0, :, ln].astype(F32).T.astype(BF16)


def _attn_prep(proj, q_gain, k_gain, cos, s1, s2, tk):
    b, t, _ = proj.shape
    nk = t // tk
    tab = pl.BlockSpec((tk, LANES), lambda bi, i: (i, 0))
    gain = pl.BlockSpec((1, LANES), lambda bi, i: (0, 0))
    return pl.pallas_call(
        _attn_prep_kernel,
        grid=(b, nk),
        in_specs=[
            pl.BlockSpec((1, tk, D_DIFF), lambda bi, i: (bi, i, COL_Q)),
            pl.BlockSpec((1, tk, D_DIFF), lambda bi, i: (bi, i, COL_KD)),
            pl.BlockSpec((1, tk, D_DIFF), lambda bi, i: (bi, i, COL_VD)),
            gain, gain, tab, tab, tab,
        ],
        out_specs=[
            pl.BlockSpec((1, D_DIFF, tk), lambda bi, i: (bi, 0, i)),
            pl.BlockSpec((1, tk, D_DIFF), lambda bi, i: (bi, i, 0)),
            pl.BlockSpec((1, 1, D_DIFF, tk), lambda bi, i: (bi, i, 0, 0)),
        ],
        out_shape=[
            jax.ShapeDtypeStruct((b, D_DIFF, t), BF16),
            jax.ShapeDtypeStruct((b, t, D_DIFF), BF16),
            jax.ShapeDtypeStruct((b, nk, D_DIFF, tk), BF16),
        ],
        compiler_params=pltpu.CompilerParams(
            dimension_semantics=("arbitrary", "arbitrary"), vmem_limit_bytes=VMEM_LIMIT),
        name="attn_prep",
    )(proj, proj, proj, q_gain, k_gain, cos, s1, s2)


def _attn_epilogue(acc, l, tq, g_ref, lam_ref, sub_ref, o_ref):
    on = acc * (1.0 / l)
    lq = lam_ref[...]
    lam = (jnp.exp(jnp.sum(lq[0:1] * lq[1:2], axis=-1, keepdims=True))
           - jnp.exp(jnp.sum(lq[2:3] * lq[3:4], axis=-1, keepdims=True)) + LAMBDA_INIT)
    o = (on[:, :tq] - lam * on[:, tq:]).T
    ms = jnp.mean(o * o, axis=-1, keepdims=True)
    y = o * lax.rsqrt(ms + RMS_EPS) * sub_ref[...] * (1.0 - LAMBDA_INIT)
    o_ref[0] = (y * _silu(g_ref[0].astype(F32))).astype(o_ref.dtype)


def _diff_attn_kernel(nk, qt_ref, k_ref, vt_ref, g_ref, lam_ref, sub_ref, o_ref,
                      sa_ref, sb_ref, pa_ref, pb_ref):
    qt = qt_ref[0]
    tq = qt.shape[1]
    tk = vt_ref.shape[3]
    sub = lax.broadcasted_iota(jnp.int32, (LANES, tq), 0)
    zero = jnp.zeros((), BF16)
    qq = jnp.concatenate([jnp.where(sub < DIFF_QK, qt, zero), jnp.where(sub >= DIFF_QK, qt, zero)],
                         axis=1)

    def scores(kb, s_ref):
        kblk = k_ref[0, pl.ds(pl.multiple_of(kb * tk, tk), tk), :]
        s_ref[...] = _dot(kblk, qq)

    def softmax_step(s_ref, p_ref, m, l):
        ms, ls, als = [], [], []
        row_chunks = [slice(r * LANES, (r + 1) * LANES) for r in range(tk // LANES)]
        for j in range(2 * tq // LANES):
            ln = slice(j * LANES, (j + 1) * LANES)
            m_new = m[:, ln]
            for rc in row_chunks:
                m_new = jnp.maximum(m_new, jnp.max(s_ref[rc, ln], axis=0, keepdims=True))
            alpha = jnp.exp2(m[:, ln] - m_new)
            l_new = alpha * l[:, ln]
            for rc in row_chunks:
                p = jnp.exp2(s_ref[rc, ln] - m_new)
                p_ref[rc, ln] = p.astype(BF16)
                l_new = l_new + jnp.sum(p, axis=0, keepdims=True)
            ms.append(m_new)
            als.append(alpha)
            ls.append(l_new)
        cat = lambda xs: jnp.concatenate(xs, axis=1)
        return cat(als), cat(ms), cat(ls)

    m = jnp.full((1, 2 * tq), -jnp.inf, F32)
    l = jnp.zeros((1, 2 * tq), F32)
    acc = jnp.zeros((LANES, 2 * tq), F32)
    scores(0, sa_ref)
    al_a, m, l = softmax_step(sa_ref, pa_ref, m, l)
    scores(1, sb_ref)

    def stage(kb, acc, alpha, m, l, s_mine, p_mine, s_other, p_other, last):
        if not last:
            scores(kb + 2, s_mine)
        pv = _dot(vt_ref[0, kb], p_mine[...])
        alpha_other, m, l = softmax_step(s_other, p_other, m, l)
        return alpha * acc + pv, alpha_other, m, l

    bufs = ((sa_ref, pa_ref), (sb_ref, pb_ref))

    def run_stages(kb0, count, carry, tail):
        m, l, acc, alpha = carry
        for u in range(count):
            (s_mine, p_mine), (s_other, p_other) = bufs[u % 2], bufs[(u + 1) % 2]
            acc, alpha, m, l = stage(kb0 + u, acc, alpha, m, l, s_mine, p_mine, s_other, p_other,
                                     tail and u >= count - 1)
        return m, l, acc, alpha

    n_loop = (nk - 2) // ATTN_UNROLL
    carry = lax.fori_loop(0, n_loop, lambda i, c: run_stages(ATTN_UNROLL * i, ATTN_UNROLL, c, False),
                          (m, l, acc, al_a))
    done = n_loop * ATTN_UNROLL
    m, l, acc, alpha = run_stages(done, nk - 1 - done, carry, True)
    acc = alpha * acc + _dot(vt_ref[0, nk - 1], bufs[(nk - 1) % 2][1][...])
    _attn_epilogue(acc, l, tq, g_ref, lam_ref, sub_ref, o_ref)


def _diff_attn_bounded_kernel(nk, qt_ref, k_ref, vt_ref, g_ref, lam_ref, sub_ref, bound_ref, o_ref):
    qt = qt_ref[0]
    tq = qt.shape[1]
    tk = vt_ref.shape[3]
    sub = lax.broadcasted_iota(jnp.int32, (LANES, tq), 0)
    zero = jnp.zeros((), BF16)
    qq = jnp.concatenate([jnp.where(sub < DIFF_QK, qt, zero), jnp.where(sub >= DIFF_QK, qt, zero)],
                         axis=1)
    strips = [slice(j * ATTN_STRIP, (j + 1) * ATTN_STRIP) for j in range(2 * tq // ATTN_STRIP)]
    q_strips = [qq[:, st] for st in strips]
    bound = bound_ref[...]

    def blocks(i, carry):
        l, acc = carry
        ls = [l[:, st] for st in strips]
        accs = [acc[:, st] for st in strips]
        kbs = [i * ATTN_BLOCKS + u for u in range(ATTN_BLOCKS)]
        s = [[_dot(k_ref[0, pl.ds(pl.multiple_of(kb * tk, tk), tk), :], q_st) for q_st in q_strips]
             for kb in kbs]
        for u, kb in enumerate(kbs):
            vblk = vt_ref[0, kb]
            for j in range(len(strips)):
                p = jnp.exp2(s[u][j] - bound)
                ls[j] = ls[j] + jnp.sum(p, axis=0, keepdims=True)
                accs[j] = accs[j] + _dot(vblk, p.astype(BF16))
        return jnp.concatenate(ls, axis=1), jnp.concatenate(accs, axis=1)

    assert nk % ATTN_BLOCKS == 0
    l, acc = lax.fori_loop(0, nk // ATTN_BLOCKS, blocks,
                           (jnp.zeros((1, 2 * tq), F32), jnp.zeros((LANES, 2 * tq), F32)))
    _attn_epilogue(acc, l, tq, g_ref, lam_ref, sub_ref, o_ref)


def _diff_attn(bounded, qt, kn, vt, proj, lambda_qk, subln, bound, tq=1024):
    b, _, t = qt.shape
    nk, tk = vt.shape[1], vt.shape[3]
    assert nk >= 2 and nk % 2 == 0
    gcol = COL_GD * (1024 // LANES)
    in_specs = [
        pl.BlockSpec((1, LANES, tq), lambda bi, h, i: (bi, h, i)),
        pl.BlockSpec((1, t, LANES), lambda bi, h, i: (bi, 0, h)),
        pl.BlockSpec((1, nk, LANES, tk), lambda bi, h, i: (bi, 0, h, 0)),
        pl.BlockSpec((1, tq, LANES), lambda bi, h, i: (bi, i, gcol + h)),
        pl.BlockSpec((4, DIFF_QK), lambda bi, h, i: (0, 0)),
        pl.BlockSpec((1, LANES), lambda bi, h, i: (0, 0)),
    ]
    args = [qt, kn, vt, proj, lambda_qk, subln]
    if bounded:
        body = functools.partial(_diff_attn_bounded_kernel, nk)
        in_specs.append(pl.BlockSpec((1, 1), lambda bi, h, i: (0, 0)))
        args.append(bound)
        scratch = []
    else:
        body = functools.partial(_diff_attn_kernel, nk)
        scratch = [pltpu.VMEM((tk, 2 * tq), F32), pltpu.VMEM((tk, 2 * tq), F32),
                   pltpu.VMEM((tk, 2 * tq), BF16), pltpu.VMEM((tk, 2 * tq), BF16)]
    return pl.pallas_call(
        body,
        grid=(b, N_DIFF_HEADS, t // tq),
        in_specs=in_specs,
        out_specs=pl.BlockSpec((1, tq, LANES), lambda bi, h, i: (bi, i, h)),
        out_shape=jax.ShapeDtypeStruct((b, t, D_DIFF), BF16),
        scratch_shapes=scratch,
        compiler_params=pltpu.CompilerParams(
            dimension_semantics=("arbitrary", "arbitrary", "arbitrary"), vmem_limit_bytes=VMEM_LIMIT),
        name="diff_attn_bounded" if bounded else "diff_attn",
    )(*args)


def _out_proj_kernel(yr_ref, yd_ref, wr_ref, wd_ref, x_ref, o_ref):
    o_ref[...] = x_ref[...] + _dot(yr_ref[...], wr_ref[...]) + _dot(yd_ref[...], wd_ref[...])


def _out_proj(y_r, y_d, w_o, x2, tm=512, tn=D_MODEL):
    n = x2.shape[0]
    return pl.pallas_call(
        _out_proj_kernel,
        grid=(n // tm, D_MODEL // tn),
        in_specs=[
            pl.BlockSpec((tm, D_RWKV), lambda i, j: (i, 0)),
            pl.BlockSpec((tm, D_DIFF), lambda i, j: (i, 0)),
            pl.BlockSpec((D_RWKV, tn), lambda i, j: (0, j)),
            pl.BlockSpec((D_DIFF, tn), lambda i, j: (1, j)),
            pl.BlockSpec((tm, tn), lambda i, j: (i, j)),
        ],
        out_specs=pl.BlockSpec((tm, tn), lambda i, j: (i, j)),
        out_shape=jax.ShapeDtypeStruct((n, D_MODEL), F32),
        compiler_params=pltpu.CompilerParams(
            dimension_semantics=("arbitrary", "arbitrary"), vmem_limit_bytes=VMEM_LIMIT),
        name="out_proj",
    )(y_r, y_d, w_o, w_o, x2)


def _rope_tables(t):
    inv = ROPE_THETA ** (-jnp.arange(ROPE_HALF, dtype=F32) * 2.0 / ROPE_DIMS)
    ang = jnp.arange(t, dtype=F32)[:, None] * inv[None, :]
    lane = jnp.arange(LANES) % DIFF_QK
    freq = jnp.arange(ROPE_HALF)[:, None]
    lo = (lane[None, :] == freq).astype(F32)
    hi = (lane[None, :] == freq + ROPE_HALF).astype(F32)
    expand = functools.partial(jnp.dot, precision=lax.Precision.HIGHEST)
    cos, sin = jnp.cos(ang), jnp.sin(ang)
    c = expand(cos, lo + hi) + (lane >= ROPE_DIMS).astype(F32)[None, :]
    return c, expand(sin, hi), expand(-sin, lo)


def _prepare_weights(norm_gain, w_in, mu_shift, w0, w_up, a0, a_up, k_k, k_a, r_k, gn_gain, gn_bias,
                     q_norm_gain, k_norm_gain, subln_gain, w_out):
    d3 = 3 * D_RWKV
    shift_cols = d3 + 2 * LORA
    w_main = jnp.concatenate([w_in[:, :d3], w_in[:, shift_cols:]], axis=1).astype(BF16)
    w_lora = w_in[:, d3:shift_cols].astype(BF16)
    row = lambda a: a.reshape(1, -1).astype(F32)
    zpad = jnp.zeros((LORA, D_RWKV), F32)
    dirs = []
    for d in range(2):
        dirs.append((
            row(0.5 * w0[d]), jnp.concatenate([0.5 * w_up[d], zpad], axis=0).astype(BF16),
            row(0.5 * a0[d]), jnp.concatenate([zpad, 0.5 * a_up[d]], axis=0).astype(BF16),
            row(k_k), row(k_a)))
    final_extra = (row(r_k), row(gn_gain), row(gn_bias))
    bound = (SCORE_BOUND_COEF * jnp.max(jnp.abs(q_norm_gain)) * jnp.max(jnp.abs(k_norm_gain))
             ).astype(F32).reshape(1, 1)
    attn = (row(jnp.tile(q_norm_gain, 2)), row(jnp.tile(k_norm_gain, 2)), row(subln_gain), bound)
    proj_w = (row(norm_gain), w_main, w_lora, row(mu_shift[:d3]), row(mu_shift[d3:shift_cols]))
    return proj_w, dirs, final_extra, attn, w_out.astype(BF16)


def _layer(x, weights, lambda_qk, rope):
    proj_w, dirs, final_extra, attn, w_o = weights
    b, t, _ = x.shape
    x2 = x.reshape(b * t, D_MODEL)
    proj2, lora2 = _in_proj(x2, t, *proj_w)
    proj = proj2.reshape(b, t, D_MAIN)
    lora = lora2.reshape(b, t, 2 * LORA)

    o_f = _rwkv_call(False, False, proj, lora, dirs[0], None)
    y_r = _rwkv_call(True, True, proj, lora, dirs[1], final_extra + (o_f,))

    tk = 512
    cos, s1, s2 = rope
    q_gain, k_gain, subln, bound = attn
    qt, kn, vt = _attn_prep(proj, q_gain, k_gain, cos, s1, s2, tk)
    y_d = lax.cond(bound[0, 0] <= ATTN_BOUND_MAX,
                   functools.partial(_diff_attn, True), functools.partial(_diff_attn, False),
                   qt, kn, vt, proj, lambda_qk, subln, bound)

    out = _out_proj(y_r.reshape(b * t, D_RWKV), y_d.reshape(b * t, D_DIFF), w_o, x2)
    return out.reshape(b, t, D_MODEL)


def kernel(x_prompt, x_sample, norm_gain, w_in, mu_shift, w0, w_up, a0, a_up, k_k, k_a, r_k, gn_gain,
           gn_bias, q_norm_gain, k_norm_gain, lambda_qk, subln_gain, w_out):
    weights = _prepare_weights(norm_gain[0], w_in[0], mu_shift[0], w0[0], w_up[0], a0[0], a_up[0],
                               k_k[0], k_a[0], r_k[0], gn_gain[0], gn_bias[0], q_norm_gain[0],
                               k_norm_gain[0], subln_gain[0], w_out[0])
    lam = lambda_qk[0].astype(F32)
    rope = _rope_tables(max(x_prompt.shape[1], x_sample.shape[1]))
    return (_layer(x_prompt, weights, lam, rope), _layer(x_sample, weights, lam, rope))
```

```python
import functools
import itertools
import math

import jax
import jax.numpy as jnp
from jax import lax
from jax.experimental import pallas as pl
from jax.experimental.pallas import tpu as pltpu

F32 = jnp.float32
BF16 = jnp.bfloat16

D_MODEL = 2048
D_RWKV = 1024
D_DIFF = 1024
RWKV_HEAD = 64
DIFF_VDIM = 128
DIFF_QK = 64
N_DIFF_HEADS = D_DIFF // DIFF_VDIM
LORA = 64
ROPE_DIMS = DIFF_QK // 4
ROPE_HALF = ROPE_DIMS // 2
ROPE_THETA = 500000.0
RMS_EPS = 1e-6
GN_EPS = 64e-5
DECAY_SCALE = 0.606531
LAMBDA_INIT = 0.8 - 0.6 * math.exp(-0.3 * 0)
LOG2E = 1.4426950408889634

LANES = 128
HALO = 8
CHUNK = 64
PAIRS = D_RWKV // LANES
D_MAIN = 8 * 1024
SHIFT_TILES = 3
SHIFT_SPLIT = 4
ATTN_UNROLL = 4
ATTN_STRIP = 256
ATTN_BLOCKS = 4
SCORE_BOUND_COEF = 1.01 * DIFF_QK * DIFF_QK ** -0.5 * LOG2E
ATTN_BOUND_MAX = 40.0
VMEM_LIMIT = 56 * 1024 * 1024

COL_R, COL_K, COL_V, COL_GR, COL_Q, COL_KD, COL_VD, COL_GD = range(8)


def _dot(a, b):
    return jnp.dot(a, b, preferred_element_type=F32)


def _dot_nt(a, b):
    return lax.dot_general(a, b, (((1,), (1,)), ((), ())), preferred_element_type=F32)


def _dot_tn(a, b):
    return lax.dot_general(a, b, (((0,), (0,)), ((), ())), preferred_element_type=F32)


def _split(x):
    hi = x.astype(BF16)
    return hi, (x - hi.astype(F32)).astype(BF16)


def _silu(x):
    h = 0.5 * x
    return h + h * jnp.tanh(h)


def _seg_ones(width):
    r = lax.broadcasted_iota(jnp.int32, (LANES, LANES), 0) // width
    c = lax.broadcasted_iota(jnp.int32, (LANES, LANES), 1) // width
    return (r == c).astype(BF16)


def _shifted(x, prev_row, next_row, mu):
    tm = x.shape[0]
    both = pltpu.roll(x, 1, 0) + pltpu.roll(x, tm - 1, 0)
    row8 = lax.broadcasted_iota(jnp.int32, (HALO, 1), 0)
    top = both[:HALO] + jnp.where(row8 == 0, prev_row - x[tm - 1:tm], 0.0)
    bot = both[tm - HALO:] + jnp.where(row8 == HALO - 1, next_row - x[0:1], 0.0)
    both = jnp.concatenate([top, both[HALO:tm - HALO], bot], axis=0)
    return x * (1.0 - mu) + both * (0.5 * mu)


def _in_proj_kernel(blocks_per_seq, x_ref, xp_ref, xn_ref, g_ref, w_ref, wl_ref, mu_ref, mul_ref,
                    o_ref, ol_ref, h_scr, hh_scr):
    i, j = pl.program_id(0), pl.program_id(1)
    pos = i % blocks_per_seq
    has_prev = (pos > 0).astype(F32)
    has_next = (pos < blocks_per_seq - 1).astype(F32)

    def norm(x):
        ms = jnp.mean(x * x, axis=-1, keepdims=True)
        return (x * lax.rsqrt(ms + RMS_EPS) * g_ref[...]).astype(BF16)

    def shifted_product(h, hh, w, mu, nsplit=1):
        edge = _dot(hh, w)
        rc = h.shape[0] // nsplit
        parts = [_dot(h[k * rc:(k + 1) * rc], w) for k in range(nsplit)]
        outs = []
        for k in range(nsplit):
            prev_row = edge[HALO - 1:HALO] * has_prev if k == 0 else parts[k - 1][rc - 1:rc]
            next_row = edge[HALO:HALO + 1] * has_next if k == nsplit - 1 else parts[k + 1][0:1]
            outs.append(_shifted(parts[k], prev_row, next_row, mu))
        return jnp.concatenate(outs, axis=0)

    @pl.when(j == 0)
    def _():
        h = norm(x_ref[...])
        hh = norm(jnp.concatenate([xp_ref[...], xn_ref[...]], axis=0))
        h_scr[...] = h
        hh_scr[...] = hh
        ol_ref[...] = shifted_product(h, hh, wl_ref[...], mul_ref[...])

    @pl.when(j < SHIFT_TILES)
    def _():
        o_ref[...] = shifted_product(h_scr[...], hh_scr[...], w_ref[...], mu_ref[...],
                                     nsplit=SHIFT_SPLIT).astype(o_ref.dtype)

    @pl.when(j >= SHIFT_TILES)
    def _():
        o_ref[...] = _dot(h_scr[...], w_ref[...]).astype(o_ref.dtype)


def _in_proj(x2, seq_len, gain, w_main, w_lora, mu_main, mu_lora, tm=1024, tn=1024):
    n = x2.shape[0]
    nh = n // HALO
    return pl.pallas_call(
        functools.partial(_in_proj_kernel, seq_len // tm),
        grid=(n // tm, D_MAIN // tn),
        in_specs=[
            pl.BlockSpec((tm, D_MODEL), lambda i, j: (i, 0)),
            pl.BlockSpec((HALO, D_MODEL), lambda i, j: (jnp.maximum(i * (tm // HALO) - 1, 0), 0)),
            pl.BlockSpec((HALO, D_MODEL), lambda i, j: (jnp.minimum((i + 1) * (tm // HALO), nh - 1), 0)),
            pl.BlockSpec((1, D_MODEL), lambda i, j: (0, 0)),
            pl.BlockSpec((D_MODEL, tn), lambda i, j: (0, j)),
            pl.BlockSpec((D_MODEL, 2 * LORA), lambda i, j: (0, 0)),
            pl.BlockSpec((1, tn), lambda i, j: (0, jnp.minimum(j, SHIFT_TILES - 1))),
            pl.BlockSpec((1, 2 * LORA), lambda i, j: (0, 0)),
        ],
        out_specs=[
            pl.BlockSpec((tm, tn), lambda i, j: (i, j)),
            pl.BlockSpec((tm, 2 * LORA), lambda i, j: (i, 0)),
        ],
        out_shape=[
            jax.ShapeDtypeStruct((n, D_MAIN), BF16),
            jax.ShapeDtypeStruct((n, 2 * LORA), F32),
        ],
        scratch_shapes=[pltpu.VMEM((tm, D_MODEL), BF16), pltpu.VMEM((2 * HALO, D_MODEL), BF16)],
        compiler_params=pltpu.CompilerParams(
            dimension_semantics=("arbitrary", "arbitrary"), vmem_limit_bytes=VMEM_LIMIT),
        name="in_proj",
    )(x2, x2, x2, gain, w_main, w_lora, mu_main, mu_lora)


def _rwkv_kernel(rev, final, tb, *refs):
    r_ref, k_ref, v_ref, l_ref, w0_ref, wup_ref, a0_ref, aup_ref, kk_ref, ka_ref = refs[:10]
    if final:
        (rk_ref, gng_ref, gnb_ref, g_ref, of_ref, out_ref,
         a_s, r_s, bt_s, kt_s, bh_s, kh_s, v_s, dec_s, ob, bonus_s, state) = refs[10:]
    else:
        out_ref, a_s, r_s, bt_s, kt_s, bh_s, kh_s, v_s, dec_s, state = refs[10:]
        ob = out_ref.at[0]

    nchunk = tb // CHUNK

    @pl.when(pl.program_id(1) == 0)
    def _():
        state[...] = jnp.zeros_like(state)

    ti = lax.broadcasted_iota(jnp.int32, (tb, tb), 0)
    tj = lax.broadcasted_iota(jnp.int32, (tb, tb), 1)
    z = l_ref[0]
    lane = lax.broadcasted_iota(jnp.int32, (tb, 2 * LORA), 1)
    zt = jnp.where(lane < LORA, jnp.tanh(z), z).astype(BF16)
    ones64 = _seg_ones(RWKV_HEAD)
    same = (ti // CHUNK) == (tj // CHUNK)
    order = (tj >= ti) if rev else (tj <= ti)
    tri_blk = (same & order).astype(BF16)
    pairs = range(PAIRS)
    lns = [slice(p * LANES, (p + 1) * LANES) for p in pairs]

    def seg(x):
        xb = x.astype(BF16)
        return jnp.concatenate([_dot(xb[:, ln], ones64) for ln in lns], axis=1)

    rr = r_ref[0].astype(F32)
    kk_s = k_ref[0].astype(F32)
    vv = v_ref[0].astype(F32)
    logw = (-0.5 * DECAY_SCALE * LOG2E) * (1.0 + jnp.tanh(w0_ref[...] + _dot(zt, wup_ref[...])))
    asig = 0.5 + 0.5 * jnp.tanh(a0_ref[...] + _dot(zt, aup_ref[...]))
    kk = kk_s * kk_ref[...]
    kk = kk * lax.rsqrt(jnp.maximum(seg(kk * kk), 1e-12))
    kd = kk_s * (1.0 + (asig - 1.0) * ka_ref[...])
    bb = asig * kk
    if final:
        bonus_s[...] = seg(rr * kk_s * rk_ref[...]) * vv

    lw_hi, lw_lo = _split(logw)
    cum = _dot(tri_blk, lw_hi) + _dot(tri_blk, lw_lo)
    g_inv = jnp.exp2(-cum)
    bt = bb * g_inv
    kt = kd * g_inv
    a_s[...] = (-kk * jnp.exp2(cum - logw)).astype(BF16)
    r_s[...] = (rr * jnp.exp2(cum)).astype(BF16)
    bt_s[...] = bt.astype(BF16)
    kt_s[...] = kt.astype(BF16)
    v_s[...] = vv.astype(BF16)
    for c in range(nchunk):
        rows = slice(c * CHUNK, (c + 1) * CHUNK)
        end = c * CHUNK if rev else (c + 1) * CHUNK - 1
        dec = jnp.exp2(cum[end:end + 1, :])
        bh_s[rows, :] = (bt[rows] * dec).astype(BF16)
        kh_s[rows, :] = (kt[rows] * dec).astype(BF16)
        dec_s[c * HALO:(c + 1) * HALO, :] = jnp.broadcast_to(dec, (HALO, D_RWKV))

    t_i = lax.broadcasted_iota(jnp.int32, (CHUNK, LANES), 0)
    j_i = lax.broadcasted_iota(jnp.int32, (CHUNK, LANES), 1) % CHUNK
    if rev:
        strict, incl = j_i > t_i, j_i >= t_i
    else:
        strict, incl = j_i < t_i, j_i <= t_i
    eye_pair = (j_i == t_i).astype(F32)
    bd_mask = (lax.broadcasted_iota(jnp.int32, (LANES, LANES), 0) // CHUNK
               == lax.broadcasted_iota(jnp.int32, (LANES, LANES), 1) // CHUNK)
    zero_b = jnp.zeros((), BF16)

    def bd(xb):
        return jnp.where(bd_mask, jnp.concatenate([xb, xb], axis=0), zero_b)

    def stack(x, y):
        return jnp.concatenate([x, y], axis=0)

    rows_of = lambda c: slice(c * CHUNK, (c + 1) * CHUNK)
    order = list(reversed(range(nchunk))) if rev else list(range(nchunk))
    par = {}

    def independent_part(chunks):
        items = [(c, p) for c in chunks for p in pairs]
        n_items = range(len(items))
        a_t = [a_s[rows_of(c), lns[p]] for c, p in items]
        r_t = [r_s[rows_of(c), lns[p]] for c, p in items]
        v_c = [v_s[rows_of(c), lns[p]] for c, p in items]
        bd_v = [bd(v) for v in v_c]
        q = [_dot_nt(stack(a_t[n], r_t[n]),
                     stack(bd(bt_s[rows_of(c), lns[p]]), bd(kt_s[rows_of(c), lns[p]])))
             for n, (c, p) in enumerate(items)]
        l_ab = [jnp.where(strict, x[:CHUNK, :LANES], 0.0) for x in q]
        l_ak = [jnp.where(strict, x[:CHUNK, LANES:], 0.0).astype(BF16) for x in q]
        m_r = [jnp.concatenate([jnp.where(incl, x[CHUNK:, :LANES], 0.0).astype(BF16),
                                jnp.where(incl, x[CHUNK:, LANES:], 0.0).astype(BF16)], axis=1) for x in q]
        yield
        lak_v = [_dot(l_ak[n], bd_v[n]) for n in n_items]
        t_inv = [eye_pair + x for x in l_ab]
        lk = [x.astype(BF16) for x in l_ab]
        lk = [_dot(x, bd(x)).astype(BF16) for x in lk]
        yield
        for _ in range(4):
            res = [_dot(stack(lk[n], t_inv[n].astype(BF16)), bd(lk[n])) for n in n_items]
            lk = [x[:CHUNK].astype(BF16) for x in res]
            t_inv = [t_inv[n] + res[n][CHUNK:] for n in n_items]
            yield
        t_inv = [(t_inv[n] + _dot(t_inv[n].astype(BF16), bd(lk[n]))).astype(BF16) for n in n_items]
        yield
        aw = [_dot(t_inv[n], jnp.concatenate([bd(a_t[n]), bd(lak_v[n].astype(BF16))], axis=1))
              for n in n_items]
        for n, key in enumerate(items):
            par[key] = (aw[n], r_t[n], m_r[n], bd_v[n], v_c[n])
        yield

    s = [state[p] for p in pairs]

    def recurrence(chunks):
        for c in chunks:
            rows = rows_of(c)
            aw, r_t, m_r, bd_v, v_c = zip(*[par[(c, p)] for p in pairs])
            uo = [_dot_nt(stack(aw[p][:, :LANES].astype(BF16), r_t[p]), s[p].astype(BF16)) for p in pairs]
            u = [(uo[p][:CHUNK] + aw[p][:, LANES:]).astype(BF16) for p in pairs]
            yield
            o = [uo[p][CHUNK:] + _dot(m_r[p], stack(bd(u[p]), bd_v[p])) for p in pairs]
            upd = [_dot_tn(stack(u[p], v_c[p]), stack(bh_s[rows, lns[p]], kh_s[rows, lns[p]]))
                   for p in pairs]
            for p in pairs:
                s[p] = s[p] * dec_s[c * HALO:c * HALO + 1, lns[p]] + jnp.where(bd_mask, upd[p], 0.0)
                ob[rows, lns[p]] = o[p]
            yield

    half = nchunk - 1
    for _ in independent_part(order[:half]):
        pass
    for _ in itertools.zip_longest(independent_part(order[half:]), recurrence(order[:half])):
        pass
    for _ in recurrence(order[half:]):
        pass
    for p in pairs:
        state[p] = s[p]

    if final:
        o_all = ob[...] + of_ref[0]
        inv_n = 1.0 / RWKV_HEAD
        cen = o_all - seg(o_all) * inv_n
        var = seg(cen * cen) * inv_n
        on = cen * lax.rsqrt(var + GN_EPS) * gng_ref[...] + gnb_ref[...]
        out_ref[0] = ((on + bonus_s[...]) * _silu(g_ref[0].astype(F32))).astype(out_ref.dtype)


def _rwkv_call(rev, final, proj, lora, params, extra, tb=256):
    b, t, _ = proj.shape
    nblk = t // tb

    def blk(i):
        return (nblk - 1 - i) if rev else i

    def main_spec(col, width=1024):
        return pl.BlockSpec((1, tb, width), lambda bi, i: (bi, blk(i), col))

    def full_spec(a):
        return pl.BlockSpec(a.shape, lambda bi, i: (0,) * a.ndim)

    in_specs = [main_spec(COL_R), main_spec(COL_K), main_spec(COL_V), main_spec(0, 2 * LORA)]
    args = [proj, proj, proj, lora]
    for a in params:
        in_specs.append(full_spec(a))
        args.append(a)
    scratch = [pltpu.VMEM((tb, D_RWKV), BF16) for _ in range(7)]
    scratch.append(pltpu.VMEM((tb // CHUNK * HALO, D_RWKV), F32))
    if final:
        rk, gng, gnb, o_f = extra
        for a in (rk, gng, gnb):
            in_specs.append(full_spec(a))
            args.append(a)
        in_specs += [main_spec(COL_GR), pl.BlockSpec((1, tb, D_RWKV), lambda bi, i: (bi, blk(i), 0))]
        args += [proj, o_f]
        scratch += [pltpu.VMEM((tb, D_RWKV), F32), pltpu.VMEM((tb, D_RWKV), F32)]
        out_dtype = BF16
    else:
        out_dtype = F32
    scratch.append(pltpu.VMEM((PAIRS, LANES, LANES), F32))
    return pl.pallas_call(
        functools.partial(_rwkv_kernel, rev, final, tb),
        grid=(b, nblk),
        in_specs=in_specs,
        out_specs=pl.BlockSpec((1, tb, D_RWKV), lambda bi, i: (bi, blk(i), 0)),
        out_shape=jax.ShapeDtypeStruct((b, t, D_RWKV), out_dtype),
        scratch_shapes=scratch,
        compiler_params=pltpu.CompilerParams(
            dimension_semantics=("arbitrary", "arbitrary"), vmem_limit_bytes=VMEM_LIMIT),
        name="rwkv_bwd" if rev else "rwkv_fwd",
    )(*args)


def _attn_prep_kernel(q_ref, k_ref, v_ref, qg_ref, kg_ref, cos_ref, s1_ref, s2_ref,
                      qt_ref, ko_ref, vt_ref):
    ones64 = _seg_ones(DIFF_QK)
    cos, s1, s2 = cos_ref[...], s1_ref[...], s2_ref[...]

    def norm_rope(x, gain):
        ms = _dot((x * x).astype(BF16), ones64) * (1.0 / DIFF_QK)
        y = x * lax.rsqrt(ms + RMS_EPS) * gain
        return y * cos + pltpu.roll(y, ROPE_HALF, 1) * s1 + pltpu.roll(y, LANES - ROPE_HALF, 1) * s2

    for h in range(N_DIFF_HEADS):
        ln = slice(h * LANES, (h + 1) * LANES)
        qh = norm_rope(q_ref[0, :, ln].astype(F32), qg_ref[...]) * (DIFF_QK ** -0.5 * LOG2E)
        qt_ref[0, ln, :] = qh.T.astype(BF16)
        ko_ref[0, :, ln] = norm_rope(k_ref[0, :, ln].astype(F32), kg_ref[...]).astype(BF16)
        vt_ref[0, 0, ln, :] = v_ref[0, :, ln].astype(F32).T.astype(BF16)


def _attn_prep(proj, q_gain, k_gain, cos, s1, s2, tk):
    b, t, _ = proj.shape
    nk = t // tk
    tab = pl.BlockSpec((tk, LANES), lambda bi, i: (i, 0))
    gain = pl.BlockSpec((1, LANES), lambda bi, i: (0, 0))
    return pl.pallas_call(
        _attn_prep_kernel,
        grid=(b, nk),
        in_specs=[
            pl.BlockSpec((1, tk, D_DIFF), lambda bi, i: (bi, i, COL_Q)),
            pl.BlockSpec((1, tk, D_DIFF), lambda bi, i: (bi, i, COL_KD)),
            pl.BlockSpec((1, tk, D_DIFF), lambda bi, i: (bi, i, COL_VD)),
            gain, gain, tab, tab, tab,
        ],
        out_specs=[
            pl.BlockSpec((1, D_DIFF, tk), lambda bi, i: (bi, 0, i)),
            pl.BlockSpec((1, tk, D_DIFF), lambda bi, i: (bi, i, 0)),
            pl.BlockSpec((1, 1, D_DIFF, tk), lambda bi, i: (bi, i, 0, 0)),
        ],
        out_shape=[
            jax.ShapeDtypeStruct((b, D_DIFF, t), BF16),
            jax.ShapeDtypeStruct((b, t, D_DIFF), BF16),
            jax.ShapeDtypeStruct((b, nk, D_DIFF, tk), BF16),
        ],
        compiler_params=pltpu.CompilerParams(
            dimension_semantics=("arbitrary", "arbitrary"), vmem_limit_bytes=VMEM_LIMIT),
        name="attn_prep",
    )(proj, proj, proj, q_gain, k_gain, cos, s1, s2)


def _attn_epilogue(acc, l, tq, g_ref, lam_ref, sub_ref, o_ref):
    on = acc * (1.0 / l)
    lq = lam_ref[...]
    lam = (jnp.exp(jnp.sum(lq[0:1] * lq[1:2], axis=-1, keepdims=True))
           - jnp.exp(jnp.sum(lq[2:3] * lq[3:4], axis=-1, keepdims=True)) + LAMBDA_INIT)
    o = (on[:, :tq] - lam * on[:, tq:]).T
    ms = jnp.mean(o * o, axis=-1, keepdims=True)
    y = o * lax.rsqrt(ms + RMS_EPS) * sub_ref[...] * (1.0 - LAMBDA_INIT)
    o_ref[0] = (y * _silu(g_ref[0].astype(F32))).astype(o_ref.dtype)


def _diff_attn_kernel(nk, qt_ref, k_ref, vt_ref, g_ref, lam_ref, sub_ref, o_ref,
                      sa_ref, sb_ref, pa_ref, pb_ref):
    qt = qt_ref[0]
    tq = qt.shape[1]
    tk = vt_ref.shape[3]
    sub = lax.broadcasted_iota(jnp.int32, (LANES, tq), 0)
    zero = jnp.zeros((), BF16)
    qq = jnp.concatenate([jnp.where(sub < DIFF_QK, qt, zero), jnp.where(sub >= DIFF_QK, qt, zero)],
                         axis=1)

    def scores(kb, s_ref):
        kblk = k_ref[0, pl.ds(pl.multiple_of(kb * tk, tk), tk), :]
        s_ref[...] = _dot(kblk, qq)

    def softmax_step(s_ref, p_ref, m, l):
        ms, ls, als = [], [], []
        row_chunks = [slice(r * LANES, (r + 1) * LANES) for r in range(tk // LANES)]
        for j in range(2 * tq // LANES):
            ln = slice(j * LANES, (j + 1) * LANES)
            m_new = m[:, ln]
            for rc in row_chunks:
                m_new = jnp.maximum(m_new, jnp.max(s_ref[rc, ln], axis=0, keepdims=True))
            alpha = jnp.exp2(m[:, ln] - m_new)
            l_new = alpha * l[:, ln]
            for rc in row_chunks:
                p = jnp.exp2(s_ref[rc, ln] - m_new)
                p_ref[rc, ln] = p.astype(BF16)
                l_new = l_new + jnp.sum(p, axis=0, keepdims=True)
            ms.append(m_new)
            als.append(alpha)
            ls.append(l_new)
        cat = lambda xs: jnp.concatenate(xs, axis=1)
        return cat(als), cat(ms), cat(ls)

    m = jnp.full((1, 2 * tq), -jnp.inf, F32)
    l = jnp.zeros((1, 2 * tq), F32)
    acc = jnp.zeros((LANES, 2 * tq), F32)
    scores(0, sa_ref)
    al_a, m, l = softmax_step(sa_ref, pa_ref, m, l)
    scores(1, sb_ref)

    def stage(kb, acc, alpha, m, l, s_mine, p_mine, s_other, p_other, last):
        if not last:
            scores(kb + 2, s_mine)
        pv = _dot(vt_ref[0, kb], p_mine[...])
        alpha_other, m, l = softmax_step(s_other, p_other, m, l)
        return alpha * acc + pv, alpha_other, m, l

    bufs = ((sa_ref, pa_ref), (sb_ref, pb_ref))

    def run_stages(kb0, count, carry, tail):
        m, l, acc, alpha = carry
        for u in range(count):
            (s_mine, p_mine), (s_other, p_other) = bufs[u % 2], bufs[(u + 1) % 2]
            acc, alpha, m, l = stage(kb0 + u, acc, alpha, m, l, s_mine, p_mine, s_other, p_other,
                                     tail and u >= count - 1)
        return m, l, acc, alpha

    n_loop = (nk - 2) // ATTN_UNROLL
    carry = lax.fori_loop(0, n_loop, lambda i, c: run_stages(ATTN_UNROLL * i, ATTN_UNROLL, c, False),
                          (m, l, acc, al_a))
    done = n_loop * ATTN_UNROLL
    m, l, acc, alpha = run_stages(done, nk - 1 - done, carry, True)
    acc = alpha * acc + _dot(vt_ref[0, nk - 1], bufs[(nk - 1) % 2][1][...])
    _attn_epilogue(acc, l, tq, g_ref, lam_ref, sub_ref, o_ref)


def _diff_attn_bounded_kernel(nk, qt_ref, k_ref, vt_ref, g_ref, lam_ref, sub_ref, bound_ref, o_ref):
    qt = qt_ref[0]
    tq = qt.shape[1]
    tk = vt_ref.shape[3]
    sub = lax.broadcasted_iota(jnp.int32, (LANES, tq), 0)
    zero = jnp.zeros((), BF16)
    qq = jnp.concatenate([jnp.where(sub < DIFF_QK, qt, zero), jnp.where(sub >= DIFF_QK, qt, zero)],
                         axis=1)
    strips = [slice(j * ATTN_STRIP, (j + 1) * ATTN_STRIP) for j in range(2 * tq // ATTN_STRIP)]
    q_strips = [qq[:, st] for st in strips]
    bound = bound_ref[...]

    def blocks(i, carry):
        l, acc = carry
        ls = [l[:, st] for st in strips]
        accs = [acc[:, st] for st in strips]
        kbs = [i * ATTN_BLOCKS + u for u in range(ATTN_BLOCKS)]
        s = [[_dot(k_ref[0, pl.ds(pl.multiple_of(kb * tk, tk), tk), :], q_st) for q_st in q_strips]
             for kb in kbs]
        for u, kb in enumerate(kbs):
            vblk = vt_ref[0, kb]
            for j in range(len(strips)):
                p = jnp.exp2(s[u][j] - bound)
                ls[j] = ls[j] + jnp.sum(p, axis=0, keepdims=True)
                accs[j] = accs[j] + _dot(vblk, p.astype(BF16))
        return jnp.concatenate(ls, axis=1), jnp.concatenate(accs, axis=1)

    assert nk % ATTN_BLOCKS == 0
    l, acc = lax.fori_loop(0, nk // ATTN_BLOCKS, blocks,
                           (jnp.zeros((1, 2 * tq), F32), jnp.zeros((LANES, 2 * tq), F32)))
    _attn_epilogue(acc, l, tq, g_ref, lam_ref, sub_ref, o_ref)


def _diff_attn(bounded, qt, kn, vt, proj, lambda_qk, subln, bound, tq=1024):
    b, _, t = qt.shape
    nk, tk = vt.shape[1], vt.shape[3]
    assert nk >= 2 and nk % 2 == 0
    gcol = COL_GD * (1024 // LANES)
    in_specs = [
        pl.BlockSpec((1, LANES, tq), lambda bi, h, i: (bi, h, i)),
        pl.BlockSpec((1, t, LANES), lambda bi, h, i: (bi, 0, h)),
        pl.BlockSpec((1, nk, LANES, tk), lambda bi, h, i: (bi, 0, h, 0)),
        pl.BlockSpec((1, tq, LANES), lambda bi, h, i: (bi, i, gcol + h)),
        pl.BlockSpec((4, DIFF_QK), lambda bi, h, i: (0, 0)),
        pl.BlockSpec((1, LANES), lambda bi, h, i: (0, 0)),
    ]
    args = [qt, kn, vt, proj, lambda_qk, subln]
    if bounded:
        body = functools.partial(_diff_attn_bounded_kernel, nk)
        in_specs.append(pl.BlockSpec((1, 1), lambda bi, h, i: (0, 0)))
        args.append(bound)
        scratch = []
    else:
        body = functools.partial(_diff_attn_kernel, nk)
        scratch = [pltpu.VMEM((tk, 2 * tq), F32), pltpu.VMEM((tk, 2 * tq), F32),
                   pltpu.VMEM((tk, 2 * tq), BF16), pltpu.VMEM((tk, 2 * tq), BF16)]
    return pl.pallas_call(
        body,
        grid=(b, N_DIFF_HEADS, t // tq),
        in_specs=in_specs,
        out_specs=pl.BlockSpec((1, tq, LANES), lambda bi, h, i: (bi, i, h)),
        out_shape=jax.ShapeDtypeStruct((b, t, D_DIFF), BF16),
        scratch_shapes=scratch,
        compiler_params=pltpu.CompilerParams(
            dimension_semantics=("arbitrary", "arbitrary", "arbitrary"), vmem_limit_bytes=VMEM_LIMIT),
        name="diff_attn_bounded" if bounded else "diff_attn",
    )(*args)


def _out_proj_kernel(yr_ref, yd_ref, wr_ref, wd_ref, x_ref, o_ref):
    o_ref[...] = x_ref[...] + _dot(yr_ref[...], wr_ref[...]) + _dot(yd_ref[...], wd_ref[...])


def _out_proj(y_r, y_d, w_o, x2, tm=512, tn=D_MODEL):
    n = x2.shape[0]
    return pl.pallas_call(
        _out_proj_kernel,
        grid=(n // tm, D_MODEL // tn),
        in_specs=[
            pl.BlockSpec((tm, D_RWKV), lambda i, j: (i, 0)),
            pl.BlockSpec((tm, D_DIFF), lambda i, j: (i, 0)),
            pl.BlockSpec((D_RWKV, tn), lambda i, j: (0, j)),
            pl.BlockSpec((D_DIFF, tn), lambda i, j: (1, j)),
            pl.BlockSpec((tm, tn), lambda i, j: (i, j)),
        ],
        out_specs=pl.BlockSpec((tm, tn), lambda i, j: (i, j)),
        out_shape=jax.ShapeDtypeStruct((n, D_MODEL), F32),
        compiler_params=pltpu.CompilerParams(
            dimension_semantics=("arbitrary", "arbitrary"), vmem_limit_bytes=VMEM_LIMIT),
        name="out_proj",
    )(y_r, y_d, w_o, w_o, x2)


def _rope_tables(t):
    inv = ROPE_THETA ** (-jnp.arange(ROPE_HALF, dtype=F32) * 2.0 / ROPE_DIMS)
    ang = jnp.arange(t, dtype=F32)[:, None] * inv[None, :]
    lane = jnp.arange(LANES) % DIFF_QK
    freq = jnp.arange(ROPE_HALF)[:, None]
    lo = (lane[None, :] == freq).astype(F32)
    hi = (lane[None, :] == freq + ROPE_HALF).astype(F32)
    expand = functools.partial(jnp.dot, precision=lax.Precision.HIGHEST)
    cos, sin = jnp.cos(ang), jnp.sin(ang)
    c = expand(cos, lo + hi) + (lane >= ROPE_DIMS).astype(F32)[None, :]
    return c, expand(sin, hi), expand(-sin, lo)


def _prepare_weights(norm_gain, w_in, mu_shift, w0, w_up, a0, a_up, k_k, k_a, r_k, gn_gain, gn_bias,
                     q_norm_gain, k_norm_gain, subln_gain, w_out):
    d3 = 3 * D_RWKV
    shift_cols = d3 + 2 * LORA
    w_main = jnp.concatenate([w_in[:, :d3], w_in[:, shift_cols:]], axis=1).astype(BF16)
    w_lora = w_in[:, d3:shift_cols].astype(BF16)
    row = lambda a: a.reshape(1, -1).astype(F32)
    zpad = jnp.zeros((LORA, D_RWKV), F32)
    dirs = []
    for d in range(2):
        dirs.append((
            row(0.5 * w0[d]), jnp.concatenate([0.5 * w_up[d], zpad], axis=0).astype(BF16),
            row(0.5 * a0[d]), jnp.concatenate([zpad, 0.5 * a_up[d]], axis=0).astype(BF16),
            row(k_k), row(k_a)))
    final_extra = (row(r_k), row(gn_gain), row(gn_bias))
    bound = (SCORE_BOUND_COEF * jnp.max(jnp.abs(q_norm_gain)) * jnp.max(jnp.abs(k_norm_gain))
             ).astype(F32).reshape(1, 1)
    attn = (row(jnp.tile(q_norm_gain, 2)), row(jnp.tile(k_norm_gain, 2)), row(subln_gain), bound)
    proj_w = (row(norm_gain), w_main, w_lora, row(mu_shift[:d3]), row(mu_shift[d3:shift_cols]))
    return proj_w, dirs, final_extra, attn, w_out.astype(BF16)


def _layer(x, weights, lambda_qk, rope):
    proj_w, dirs, final_extra, attn, w_o = weights
    b, t, _ = x.shape
    x2 = x.reshape(b * t, D_MODEL)
    proj2, lora2 = _in_proj(x2, t, *proj_w)
    proj = proj2.reshape(b, t, D_MAIN)
    lora = lora2.reshape(b, t, 2 * LORA)

    o_f = _rwkv_call(False, False, proj, lora, dirs[0], None)
    y_r = _rwkv_call(True, True, proj, lora, dirs[1], final_extra + (o_f,))

    tk = 512
    cos, s1, s2 = rope
    q_gain, k_gain, subln, bound = attn
    qt, kn, vt = _attn_prep(proj, q_gain, k_gain, cos, s1, s2, tk)
    y_d = lax.cond(bound[0, 0] <= ATTN_BOUND_MAX,
                   functools.partial(_diff_attn, True), functools.partial(_diff_attn, False),
                   qt, kn, vt, proj, lambda_qk, subln, bound)

    out = _out_proj(y_r.reshape(b * t, D_RWKV), y_d.reshape(b * t, D_DIFF), w_o, x2)
    return out.reshape(b, t, D_MODEL)


def kernel(x_prompt, x_sample, norm_gain, w_in, mu_shift, w0, w_up, a0, a_up, k_k, k_a, r_k, gn_gain,
           gn_bias, q_norm_gain, k_norm_gain, lambda_qk, subln_gain, w_out):
    weights = _prepare_weights(norm_gain[0], w_in[0], mu_shift[0], w0[0], w_up[0], a0[0], a_up[0],
                               k_k[0], k_a[0], r_k[0], gn_gain[0], gn_bias[0], q_norm_gain[0],
                               k_norm_gain[0], subln_gain[0], w_out[0])
    lam = lambda_qk[0].astype(F32)
    rope = _rope_tables(max(x_prompt.shape[1], x_sample.shape[1]))
    return (_layer(x_prompt, weights, lam, rope), _layer(x_sample, weights, lam, rope))
```

```python
import functools
import itertools
import math

import jax
import jax.numpy as jnp
from jax import lax
from jax.experimental import pallas as pl
from jax.experimental.pallas import tpu as pltpu

F32 = jnp.float32
BF16 = jnp.bfloat16

D_MODEL = 2048
D_RWKV = 1024
D_DIFF = 1024
RWKV_HEAD = 64
DIFF_VDIM = 128
DIFF_QK = 64
N_DIFF_HEADS = D_DIFF // DIFF_VDIM
LORA = 64
ROPE_DIMS = DIFF_QK // 4
ROPE_HALF = ROPE_DIMS // 2
ROPE_THETA = 500000.0
RMS_EPS = 1e-6
GN_EPS = 64e-5
DECAY_SCALE = 0.606531
LAMBDA_INIT = 0.8 - 0.6 * math.exp(-0.3 * 0)
LOG2E = 1.4426950408889634

LANES = 128
HALO = 8
CHUNK = 64
PAIRS = D_RWKV // LANES
D_MAIN = 8 * 1024
SHIFT_TILES = 3
SHIFT_SPLIT = 4
ATTN_UNROLL = 4
ATTN_STRIP = 256
ATTN_BLOCKS = 4
SCORE_BOUND_COEF = 1.01 * DIFF_QK * DIFF_QK ** -0.5 * LOG2E
ATTN_BOUND_MAX = 40.0
VMEM_LIMIT = 56 * 1024 * 1024

COL_R, COL_K, COL_V, COL_GR, COL_Q, COL_KD, COL_VD, COL_GD = range(8)


def _dot(a, b):
    return jnp.dot(a, b, preferred_element_type=F32)


def _dot_nt(a, b):
    return lax.dot_general(a, b, (((1,), (1,)), ((), ())), preferred_element_type=F32)


def _dot_tn(a, b):
    return lax.dot_general(a, b, (((0,), (0,)), ((), ())), preferred_element_type=F32)


def _split(x):
    hi = x.astype(BF16)
    return hi, (x - hi.astype(F32)).astype(BF16)


def _silu(x):
    h = 0.5 * x
    return h + h * jnp.tanh(h)


def _seg_ones(width):
    r = lax.broadcasted_iota(jnp.int32, (LANES, LANES), 0) // width
    c = lax.broadcasted_iota(jnp.int32, (LANES, LANES), 1) // width
    return (r == c).astype(BF16)


def _shifted(x, prev_row, next_row, mu):
    tm = x.shape[0]
    both = pltpu.roll(x, 1, 0) + pltpu.roll(x, tm - 1, 0)
    row8 = lax.broadcasted_iota(jnp.int32, (HALO, 1), 0)
    top = both[:HALO] + jnp.where(row8 == 0, prev_row - x[tm - 1:tm], 0.0)
    bot = both[tm - HALO:] + jnp.where(row8 == HALO - 1, next_row - x[0:1], 0.0)
    both = jnp.concatenate([top, both[HALO:tm - HALO], bot], axis=0)
    return x * (1.0 - mu) + both * (0.5 * mu)


def _in_proj_kernel(blocks_per_seq, x_ref, xp_ref, xn_ref, g_ref, w_ref, wl_ref, mu_ref, mul_ref,
                    o_ref, ol_ref, h_scr, hh_scr):
    i, j = pl.program_id(0), pl.program_id(1)
    pos = i % blocks_per_seq
    has_prev = (pos > 0).astype(F32)
    has_next = (pos < blocks_per_seq - 1).astype(F32)

    def norm(x):
        ms = jnp.mean(x * x, axis=-1, keepdims=True)
        return (x * lax.rsqrt(ms + RMS_EPS) * g_ref[...]).astype(BF16)

    def shifted_product(h, hh, w, mu, nsplit=1):
        edge = _dot(hh, w)
        rc = h.shape[0] // nsplit
        parts = [_dot(h[k * rc:(k + 1) * rc], w) for k in range(nsplit)]
        outs = []
        for k in range(nsplit):
            prev_row = edge[HALO - 1:HALO] * has_prev if k == 0 else parts[k - 1][rc - 1:rc]
            next_row = edge[HALO:HALO + 1] * has_next if k == nsplit - 1 else parts[k + 1][0:1]
            outs.append(_shifted(parts[k], prev_row, next_row, mu))
        return jnp.concatenate(outs, axis=0)

    @pl.when(j == 0)
    def _():
        h = norm(x_ref[...])
        hh = norm(jnp.concatenate([xp_ref[...], xn_ref[...]], axis=0))
        h_scr[...] = h
        hh_scr[...] = hh
        ol_ref[...] = shifted_product(h, hh, wl_ref[...], mul_ref[...])

    @pl.when(j < SHIFT_TILES)
    def _():
        o_ref[...] = shifted_product(h_scr[...], hh_scr[...], w_ref[...], mu_ref[...],
                                     nsplit=SHIFT_SPLIT).astype(o_ref.dtype)

    @pl.when(j >= SHIFT_TILES)
    def _():
        o_ref[...] = _dot(h_scr[...], w_ref[...]).astype(o_ref.dtype)


def _in_proj(x2, seq_len, gain, w_main, w_lora, mu_main, mu_lora, tm=1024, tn=1024):
    n = x2.shape[0]
    nh = n // HALO
    return pl.pallas_call(
        functools.partial(_in_proj_kernel, seq_len // tm),
        grid=(n // tm, D_MAIN // tn),
        in_specs=[
            pl.BlockSpec((tm, D_MODEL), lambda i, j: (i, 0)),
            pl.BlockSpec((HALO, D_MODEL), lambda i, j: (jnp.maximum(i * (tm // HALO) - 1, 0), 0)),
            pl.BlockSpec((HALO, D_MODEL), lambda i, j: (jnp.minimum((i + 1) * (tm // HALO), nh - 1), 0)),
            pl.BlockSpec((1, D_MODEL), lambda i, j: (0, 0)),
            pl.BlockSpec((D_MODEL, tn), lambda i, j: (0, j)),
            pl.BlockSpec((D_MODEL, 2 * LORA), lambda i, j: (0, 0)),
            pl.BlockSpec((1, tn), lambda i, j: (0, jnp.minimum(j, SHIFT_TILES - 1))),
            pl.BlockSpec((1, 2 * LORA), lambda i, j: (0, 0)),
        ],
        out_specs=[
            pl.BlockSpec((tm, tn), lambda i, j: (i, j)),
            pl.BlockSpec((tm, 2 * LORA), lambda i, j: (i, 0)),
        ],
        out_shape=[
            jax.ShapeDtypeStruct((n, D_MAIN), BF16),
            jax.ShapeDtypeStruct((n, 2 * LORA), F32),
        ],
        scratch_shapes=[pltpu.VMEM((tm, D_MODEL), BF16), pltpu.VMEM((2 * HALO, D_MODEL), BF16)],
        compiler_params=pltpu.CompilerParams(
            dimension_semantics=("arbitrary", "arbitrary"), vmem_limit_bytes=VMEM_LIMIT),
        name="in_proj",
    )(x2, x2, x2, gain, w_main, w_lora, mu_main, mu_lora)


def _rwkv_kernel(rev, final, tb, *refs):
    r_ref, k_ref, v_ref, l_ref, w0_ref, wup_ref, a0_ref, aup_ref, kk_ref, ka_ref = refs[:10]
    if final:
        (rk_ref, gng_ref, gnb_ref, g_ref, of_ref, out_ref,
         a_s, r_s, bt_s, kt_s, bh_s, kh_s, v_s, dec_s, ob, bonus_s, state) = refs[10:]
    else:
        out_ref, a_s, r_s, bt_s, kt_s, bh_s, kh_s, v_s, dec_s, state = refs[10:]
        ob = out_ref.at[0]

    nchunk = tb // CHUNK

    @pl.when(pl.program_id(1) == 0)
    def _():
        state[...] = jnp.zeros_like(state)

    ti = lax.broadcasted_iota(jnp.int32, (tb, tb), 0)
    tj = lax.broadcasted_iota(jnp.int32, (tb, tb), 1)
    z = l_ref[0]
    lane = lax.broadcasted_iota(jnp.int32, (tb, 2 * LORA), 1)
    zt = jnp.where(lane < LORA, jnp.tanh(z), z).astype(BF16)
    ones64 = _seg_ones(RWKV_HEAD)
    same = (ti // CHUNK) == (tj // CHUNK)
    order = (tj >= ti) if rev else (tj <= ti)
    tri_blk = (same & order).astype(BF16)
    pairs = range(PAIRS)
    lns = [slice(p * LANES, (p + 1) * LANES) for p in pairs]

    def seg(x):
        xb = x.astype(BF16)
        return jnp.concatenate([_dot(xb[:, ln], ones64) for ln in lns], axis=1)

    rr = r_ref[0].astype(F32)
    kk_s = k_ref[0].astype(F32)
    vv = v_ref[0].astype(F32)
    logw = (-0.5 * DECAY_SCALE * LOG2E) * (1.0 + jnp.tanh(w0_ref[...] + _dot(zt, wup_ref[...])))
    asig = 0.5 + 0.5 * jnp.tanh(a0_ref[...] + _dot(zt, aup_ref[...]))
    kk = kk_s * kk_ref[...]
    kk = kk * lax.rsqrt(jnp.maximum(seg(kk * kk), 1e-12))
    kd = kk_s * (1.0 + (asig - 1.0) * ka_ref[...])
    bb = asig * kk
    if final:
        bonus_s[...] = seg(rr * kk_s * rk_ref[...]) * vv

    lw_hi, lw_lo = _split(logw)
    cum = _dot(tri_blk, lw_hi) + _dot(tri_blk, lw_lo)
    g_inv = jnp.exp2(-cum)
    bt = bb * g_inv
    kt = kd * g_inv
    a_s[...] = (-kk * jnp.exp2(cum - logw)).astype(BF16)
    r_s[...] = (rr * jnp.exp2(cum)).astype(BF16)
    bt_s[...] = bt.astype(BF16)
    kt_s[...] = kt.astype(BF16)
    v_s[...] = vv.astype(BF16)
    for c in range(nchunk):
        rows = slice(c * CHUNK, (c + 1) * CHUNK)
        end = c * CHUNK if rev else (c + 1) * CHUNK - 1
        dec = jnp.exp2(cum[end:end + 1, :])
        bh_s[rows, :] = (bt[rows] * dec).astype(BF16)
        kh_s[rows, :] = (kt[rows] * dec).astype(BF16)
        dec_s[c * HALO:(c + 1) * HALO, :] = jnp.broadcast_to(dec, (HALO, D_RWKV))

    t_i = lax.broadcasted_iota(jnp.int32, (CHUNK, LANES), 0)
    j_i = lax.broadcasted_iota(jnp.int32, (CHUNK, LANES), 1) % CHUNK
    if rev:
        strict, incl = j_i > t_i, j_i >= t_i
    else:
        strict, incl = j_i < t_i, j_i <= t_i
    eye_pair = (j_i == t_i).astype(F32)
    bd_mask = (lax.broadcasted_iota(jnp.int32, (LANES, LANES), 0) // CHUNK
               == lax.broadcasted_iota(jnp.int32, (LANES, LANES), 1) // CHUNK)
    zero_b = jnp.zeros((), BF16)

    def bd(xb):
        return jnp.where(bd_mask, jnp.concatenate([xb, xb], axis=0), zero_b)

    def stack(x, y):
        return jnp.concatenate([x, y], axis=0)

    rows_of = lambda c: slice(c * CHUNK, (c + 1) * CHUNK)
    order = list(reversed(range(nchunk))) if rev else list(range(nchunk))
    par = {}

    def independent_part(chunks):
        items = [(c, p) for c in chunks for p in pairs]
        n_items = range(len(items))
        a_t = [a_s[rows_of(c), lns[p]] for c, p in items]
        r_t = [r_s[rows_of(c), lns[p]] for c, p in items]
        v_c = [v_s[rows_of(c), lns[p]] for c, p in items]
        bd_v = [bd(v) for v in v_c]
        q = [_dot_nt(stack(a_t[n], r_t[n]),
                     stack(bd(bt_s[rows_of(c), lns[p]]), bd(kt_s[rows_of(c), lns[p]])))
             for n, (c, p) in enumerate(items)]
        l_ab = [jnp.where(strict, x[:CHUNK, :LANES], 0.0) for x in q]
        l_ak = [jnp.where(strict, x[:CHUNK, LANES:], 0.0).astype(BF16) for x in q]
        m_r = [jnp.concatenate([jnp.where(incl, x[CHUNK:, :LANES], 0.0).astype(BF16),
                                jnp.where(incl, x[CHUNK:, LANES:], 0.0).astype(BF16)], axis=1) for x in q]
        yield
        lak_v = [_dot(l_ak[n], bd_v[n]) for n in n_items]
        t_inv = [eye_pair + x for x in l_ab]
        lk = [x.astype(BF16) for x in l_ab]
        lk = [_dot(x, bd(x)).astype(BF16) for x in lk]
        yield
        for _ in range(4):
            res = [_dot(stack(lk[n], t_inv[n].astype(BF16)), bd(lk[n])) for n in n_items]
            lk = [x[:CHUNK].astype(BF16) for x in res]
            t_inv = [t_inv[n] + res[n][CHUNK:] for n in n_items]
            yield
        t_inv = [(t_inv[n] + _dot(t_inv[n].astype(BF16), bd(lk[n]))).astype(BF16) for n in n_items]
        yield
        aw = [_dot(t_inv[n], jnp.concatenate([bd(a_t[n]), bd(lak_v[n].astype(BF16))], axis=1))
              for n in n_items]
        for n, key in enumerate(items):
            par[key] = (aw[n], r_t[n], m_r[n], bd_v[n], v_c[n])
        yield

    s = [state[p] for p in pairs]

    def recurrence(chunks):
        for c in chunks:
            rows = rows_of(c)
            aw, r_t, m_r, bd_v, v_c = zip(*[par[(c, p)] for p in pairs])
            uo = [_dot_nt(stack(aw[p][:, :LANES].astype(BF16), r_t[p]), s[p].astype(BF16)) for p in pairs]
            u = [(uo[p][:CHUNK] + aw[p][:, LANES:]).astype(BF16) for p in pairs]
            yield
            o = [uo[p][CHUNK:] + _dot(m_r[p], stack(bd(u[p]), bd_v[p])) for p in pairs]
            upd = [_dot_tn(stack(u[p], v_c[p]), stack(bh_s[rows, lns[p]], kh_s[rows, lns[p]]))
                   for p in pairs]
            for p in pairs:
                s[p] = s[p] * dec_s[c * HALO:c * HALO + 1, lns[p]] + jnp.where(bd_mask, upd[p], 0.0)
                ob[rows, lns[p]] = o[p]
            yield

    half = nchunk - 1
    for _ in independent_part(order[:half]):
        pass
    for _ in itertools.zip_longest(independent_part(order[half:]), recurrence(order[:half])):
        pass
    for _ in recurrence(order[half:]):
        pass
    for p in pairs:
        state[p] = s[p]

    if final:
        o_all = ob[...] + of_ref[0]
        inv_n = 1.0 / RWKV_HEAD
        cen = o_all - seg(o_all) * inv_n
        var = seg(cen * cen) * inv_n
        on = cen * lax.rsqrt(var + GN_EPS) * gng_ref[...] + gnb_ref[...]
        out_ref[0] = ((on + bonus_s[...]) * _silu(g_ref[0].astype(F32))).astype(out_ref.dtype)


def _rwkv_call(rev, final, proj, lora, params, extra, tb=256):
    b, t, _ = proj.shape
    nblk = t // tb

    def blk(i):
        return (nblk - 1 - i) if rev else i

    def main_spec(col, width=1024):
        return pl.BlockSpec((1, tb, width), lambda bi, i: (bi, blk(i), col))

    def full_spec(a):
        return pl.BlockSpec(a.shape, lambda bi, i: (0,) * a.ndim)

    in_specs = [main_spec(COL_R), main_spec(COL_K), main_spec(COL_V), main_spec(0, 2 * LORA)]
    args = [proj, proj, proj, lora]
    for a in params:
        in_specs.append(full_spec(a))
        args.append(a)
    scratch = [pltpu.VMEM((tb, D_RWKV), BF16) for _ in range(7)]
    scratch.append(pltpu.VMEM((tb // CHUNK * HALO, D_RWKV), F32))
    if final:
        rk, gng, gnb, o_f = extra
        for a in (rk, gng, gnb):
            in_specs.append(full_spec(a))
            args.append(a)
        in_specs += [main_spec(COL_GR), pl.BlockSpec((1, tb, D_RWKV), lambda bi, i: (bi, blk(i), 0))]
        args += [proj, o_f]
        scratch += [pltpu.VMEM((tb, D_RWKV), F32), pltpu.VMEM((tb, D_RWKV), F32)]
        out_dtype = BF16
    else:
        out_dtype = F32
    scratch.append(pltpu.VMEM((PAIRS, LANES, LANES), F32))
    return pl.pallas_call(
        functools.partial(_rwkv_kernel, rev, final, tb),
        grid=(b, nblk),
        in_specs=in_specs,
        out_specs=pl.BlockSpec((1, tb, D_RWKV), lambda bi, i: (bi, blk(i), 0)),
        out_shape=jax.ShapeDtypeStruct((b, t, D_RWKV), out_dtype),
        scratch_shapes=scratch,
        compiler_params=pltpu.CompilerParams(
            dimension_semantics=("arbitrary", "arbitrary"), vmem_limit_bytes=VMEM_LIMIT),
        name="rwkv_bwd" if rev else "rwkv_fwd",
    )(*args)


def _attn_prep_kernel(q_ref, k_ref, qg_ref, kg_ref, cos_ref, sin_ref, qt_ref, ko_ref):
    ones64 = _seg_ones(DIFF_QK)
    cos, sin = cos_ref[...], sin_ref[...]
    r = lax.broadcasted_iota(jnp.int32, (LANES, LANES), 0)
    c = lax.broadcasted_iota(jnp.int32, (LANES, LANES), 1)
    cd = c % DIFF_QK
    partner = (((cd < ROPE_HALF) & (r == c + ROPE_HALF))
               | ((cd >= ROPE_HALF) & (cd < ROPE_DIMS) & (r == c - ROPE_HALF))).astype(BF16)

    def norm_rope(x, gain):
        ms = _dot((x * x).astype(BF16), ones64) * (1.0 / DIFF_QK)
        y = x * lax.rsqrt(ms + RMS_EPS) * gain
        return y * cos + _dot(y.astype(BF16), partner) * sin

    for h in range(N_DIFF_HEADS):
        ln = slice(h * LANES, (h + 1) * LANES)
        qh = norm_rope(q_ref[0, :, ln].astype(F32), qg_ref[...]) * (DIFF_QK ** -0.5 * LOG2E)
        qt_ref[0, ln, :] = qh.T.astype(BF16)
        ko_ref[0, :, ln] = norm_rope(k_ref[0, :, ln].astype(F32), kg_ref[...]).astype(BF16)


def _attn_prep(proj, q_gain, k_gain, cos, sin, tk):
    b, t, _ = proj.shape
    nk = t // tk
    tab = pl.BlockSpec((tk, LANES), lambda bi, i: (i, 0))
    gain = pl.BlockSpec((1, LANES), lambda bi, i: (0, 0))
    return pl.pallas_call(
        _attn_prep_kernel,
        grid=(b, nk),
        in_specs=[
            pl.BlockSpec((1, tk, D_DIFF), lambda bi, i: (bi, i, COL_Q)),
            pl.BlockSpec((1, tk, D_DIFF), lambda bi, i: (bi, i, COL_KD)),
            gain, gain, tab, tab,
        ],
        out_specs=[
            pl.BlockSpec((1, D_DIFF, tk), lambda bi, i: (bi, 0, i)),
            pl.BlockSpec((1, tk, D_DIFF), lambda bi, i: (bi, i, 0)),
        ],
        out_shape=[
            jax.ShapeDtypeStruct((b, D_DIFF, t), BF16),
            jax.ShapeDtypeStruct((b, t, D_DIFF), BF16),
        ],
        compiler_params=pltpu.CompilerParams(
            dimension_semantics=("arbitrary", "arbitrary"), vmem_limit_bytes=VMEM_LIMIT),
        name="attn_prep",
    )(proj, proj, q_gain, k_gain, cos, sin)


def _attn_epilogue(acc, l, tq, g_ref, lam_ref, sub_ref, o_ref):
    on = acc * (1.0 / l)
    lq = lam_ref[...]
    lam = (jnp.exp(jnp.sum(lq[0:1] * lq[1:2], axis=-1, keepdims=True))
           - jnp.exp(jnp.sum(lq[2:3] * lq[3:4], axis=-1, keepdims=True)) + LAMBDA_INIT)
    o = (on[:, :tq] - lam * on[:, tq:]).T
    ms = jnp.mean(o * o, axis=-1, keepdims=True)
    y = o * lax.rsqrt(ms + RMS_EPS) * sub_ref[...] * (1.0 - LAMBDA_INIT)
    o_ref[0] = (y * _silu(g_ref[0].astype(F32))).astype(o_ref.dtype)


def _diff_attn_kernel(nk, tk, qt_ref, k_ref, v_ref, g_ref, lam_ref, sub_ref, o_ref,
                      sa_ref, sb_ref, pa_ref, pb_ref):
    qt = qt_ref[0]
    tq = qt.shape[1]
    sub = lax.broadcasted_iota(jnp.int32, (LANES, tq), 0)
    zero = jnp.zeros((), BF16)
    qq = jnp.concatenate([jnp.where(sub < DIFF_QK, qt, zero), jnp.where(sub >= DIFF_QK, qt, zero)],
                         axis=1)

    def rows(kb):
        return pl.ds(pl.multiple_of(kb * tk, tk), tk)

    def scores(kb, s_ref):
        s_ref[...] = _dot(k_ref[0, rows(kb), :], qq)

    def softmax_step(s_ref, p_ref, m, l):
        ms, ls, als = [], [], []
        row_chunks = [slice(r * LANES, (r + 1) * LANES) for r in range(tk // LANES)]
        for j in range(2 * tq // LANES):
            ln = slice(j * LANES, (j + 1) * LANES)
            m_new = m[:, ln]
            for rc in row_chunks:
                m_new = jnp.maximum(m_new, jnp.max(s_ref[rc, ln], axis=0, keepdims=True))
            alpha = jnp.exp2(m[:, ln] - m_new)
            l_new = alpha * l[:, ln]
            for rc in row_chunks:
                p = jnp.exp2(s_ref[rc, ln] - m_new)
                p_ref[rc, ln] = p.astype(BF16)
                l_new = l_new + jnp.sum(p, axis=0, keepdims=True)
            ms.append(m_new)
            als.append(alpha)
            ls.append(l_new)
        cat = lambda xs: jnp.concatenate(xs, axis=1)
        return cat(als), cat(ms), cat(ls)

    m = jnp.full((1, 2 * tq), -jnp.inf, F32)
    l = jnp.zeros((1, 2 * tq), F32)
    acc = jnp.zeros((LANES, 2 * tq), F32)
    scores(0, sa_ref)
    al_a, m, l = softmax_step(sa_ref, pa_ref, m, l)
    scores(1, sb_ref)

    def stage(kb, acc, alpha, m, l, s_mine, p_mine, s_other, p_other, last):
        if not last:
            scores(kb + 2, s_mine)
        pv = _dot_tn(v_ref[0, rows(kb), :], p_mine[...])
        alpha_other, m, l = softmax_step(s_other, p_other, m, l)
        return alpha * acc + pv, alpha_other, m, l

    bufs = ((sa_ref, pa_ref), (sb_ref, pb_ref))

    def run_stages(kb0, count, carry, tail):
        m, l, acc, alpha = carry
        for u in range(count):
            (s_mine, p_mine), (s_other, p_other) = bufs[u % 2], bufs[(u + 1) % 2]
            acc, alpha, m, l = stage(kb0 + u, acc, alpha, m, l, s_mine, p_mine, s_other, p_other,
                                     tail and u >= count - 1)
        return m, l, acc, alpha

    n_loop = (nk - 2) // ATTN_UNROLL
    carry = lax.fori_loop(0, n_loop, lambda i, c: run_stages(ATTN_UNROLL * i, ATTN_UNROLL, c, False),
                          (m, l, acc, al_a))
    done = n_loop * ATTN_UNROLL
    m, l, acc, alpha = run_stages(done, nk - 1 - done, carry, True)
    acc = alpha * acc + _dot_tn(v_ref[0, rows(nk - 1), :], bufs[(nk - 1) % 2][1][...])
    _attn_epilogue(acc, l, tq, g_ref, lam_ref, sub_ref, o_ref)


def _diff_attn_bounded_kernel(nk, tk, qt_ref, k_ref, v_ref, g_ref, lam_ref, sub_ref, bound_ref, o_ref):
    qt = qt_ref[0]
    tq = qt.shape[1]
    sub = lax.broadcasted_iota(jnp.int32, (LANES, tq), 0)
    zero = jnp.zeros((), BF16)
    qq = jnp.concatenate([jnp.where(sub < DIFF_QK, qt, zero), jnp.where(sub >= DIFF_QK, qt, zero)],
                         axis=1)
    strips = [slice(j * ATTN_STRIP, (j + 1) * ATTN_STRIP) for j in range(2 * tq // ATTN_STRIP)]
    q_strips = [qq[:, st] for st in strips]
    bound = bound_ref[...]

    def blocks(i, carry):
        l, acc = carry
        ls = [l[:, st] for st in strips]
        accs = [acc[:, st] for st in strips]
        kbs = [i * ATTN_BLOCKS + u for u in range(ATTN_BLOCKS)]
        rows = [pl.ds(pl.multiple_of(kb * tk, tk), tk) for kb in kbs]
        s = [[_dot(k_ref[0, r, :], q_st) for q_st in q_strips] for r in rows]
        for u in range(ATTN_BLOCKS):
            vblk = v_ref[0, rows[u], :]
            for j in range(len(strips)):
                p = jnp.exp2(s[u][j] - bound)
                ls[j] = ls[j] + jnp.sum(p, axis=0, keepdims=True)
                accs[j] = accs[j] + _dot_tn(vblk, p.astype(BF16))
        return jnp.concatenate(ls, axis=1), jnp.concatenate(accs, axis=1)

    assert nk % ATTN_BLOCKS == 0
    l, acc = lax.fori_loop(0, nk // ATTN_BLOCKS, blocks,
                           (jnp.zeros((1, 2 * tq), F32), jnp.zeros((LANES, 2 * tq), F32)))
    _attn_epilogue(acc, l, tq, g_ref, lam_ref, sub_ref, o_ref)


def _diff_attn(bounded, qt, kn, proj, lambda_qk, subln, bound, tq=1024, tk=512):
    b, _, t = qt.shape
    nk = t // tk
    assert nk >= 2 and nk % 2 == 0
    gcol = COL_GD * (1024 // LANES)
    vcol = COL_VD * (1024 // LANES)
    in_specs = [
        pl.BlockSpec((1, LANES, tq), lambda bi, h, i: (bi, h, i)),
        pl.BlockSpec((1, t, LANES), lambda bi, h, i: (bi, 0, h)),
        pl.BlockSpec((1, t, LANES), lambda bi, h, i: (bi, 0, vcol + h)),
        pl.BlockSpec((1, tq, LANES), lambda bi, h, i: (bi, i, gcol + h)),
        pl.BlockSpec((4, DIFF_QK), lambda bi, h, i: (0, 0)),
        pl.BlockSpec((1, LANES), lambda bi, h, i: (0, 0)),
    ]
    args = [qt, kn, proj, proj, lambda_qk, subln]
    if bounded:
        body = functools.partial(_diff_attn_bounded_kernel, nk, tk)
        in_specs.append(pl.BlockSpec((1, 1), lambda bi, h, i: (0, 0)))
        args.append(bound)
        scratch = []
    else:
        body = functools.partial(_diff_attn_kernel, nk, tk)
        scratch = [pltpu.VMEM((tk, 2 * tq), F32), pltpu.VMEM((tk, 2 * tq), F32),
                   pltpu.VMEM((tk, 2 * tq), BF16), pltpu.VMEM((tk, 2 * tq), BF16)]
    return pl.pallas_call(
        body,
        grid=(b, N_DIFF_HEADS, t // tq),
        in_specs=in_specs,
        out_specs=pl.BlockSpec((1, tq, LANES), lambda bi, h, i: (bi, i, h)),
        out_shape=jax.ShapeDtypeStruct((b, t, D_DIFF), BF16),
        scratch_shapes=scratch,
        compiler_params=pltpu.CompilerParams(
            dimension_semantics=("arbitrary", "arbitrary", "arbitrary"), vmem_limit_bytes=VMEM_LIMIT),
        name="diff_attn_bounded" if bounded else "diff_attn",
    )(*args)


def _out_proj_kernel(yr_ref, yd_ref, wr_ref, wd_ref, x_ref, o_ref):
    o_ref[...] = x_ref[...] + _dot(yr_ref[...], wr_ref[...]) + _dot(yd_ref[...], wd_ref[...])


def _out_proj(y_r, y_d, w_o, x2, tm=512, tn=D_MODEL):
    n = x2.shape[0]
    return pl.pallas_call(
        _out_proj_kernel,
        grid=(n // tm, D_MODEL // tn),
        in_specs=[
            pl.BlockSpec((tm, D_RWKV), lambda i, j: (i, 0)),
            pl.BlockSpec((tm, D_DIFF), lambda i, j: (i, 0)),
            pl.BlockSpec((D_RWKV, tn), lambda i, j: (0, j)),
            pl.BlockSpec((D_DIFF, tn), lambda i, j: (1, j)),
            pl.BlockSpec((tm, tn), lambda i, j: (i, j)),
        ],
        out_specs=pl.BlockSpec((tm, tn), lambda i, j: (i, j)),
        out_shape=jax.ShapeDtypeStruct((n, D_MODEL), F32),
        compiler_params=pltpu.CompilerParams(
            dimension_semantics=("arbitrary", "arbitrary"), vmem_limit_bytes=VMEM_LIMIT),
        name="out_proj",
    )(y_r, y_d, w_o, w_o, x2)


def _rope_tables(t):
    inv = ROPE_THETA ** (-jnp.arange(ROPE_HALF, dtype=F32) * 2.0 / ROPE_DIMS)
    ang = jnp.arange(t, dtype=F32)[:, None] * inv[None, :]
    lane = jnp.arange(LANES) % DIFF_QK
    freq = jnp.arange(ROPE_HALF)[:, None]
    lo = (lane[None, :] == freq).astype(F32)
    hi = (lane[None, :] == freq + ROPE_HALF).astype(F32)
    expand = functools.partial(jnp.dot, precision=lax.Precision.HIGHEST)
    cos, sin = jnp.cos(ang), jnp.sin(ang)
    c = expand(cos, lo + hi) + (lane >= ROPE_DIMS).astype(F32)[None, :]
    return c, expand(sin, hi - lo)


def _prepare_weights(norm_gain, w_in, mu_shift, w0, w_up, a0, a_up, k_k, k_a, r_k, gn_gain, gn_bias,
                     q_norm_gain, k_norm_gain, subln_gain, w_out):
    d3 = 3 * D_RWKV
    shift_cols = d3 + 2 * LORA
    w_main = jnp.concatenate([w_in[:, :d3], w_in[:, shift_cols:]], axis=1).astype(BF16)
    w_lora = w_in[:, d3:shift_cols].astype(BF16)
    row = lambda a: a.reshape(1, -1).astype(F32)
    zpad = jnp.zeros((LORA, D_RWKV), F32)
    dirs = []
    for d in range(2):
        dirs.append((
            row(0.5 * w0[d]), jnp.concatenate([0.5 * w_up[d], zpad], axis=0).astype(BF16),
            row(0.5 * a0[d]), jnp.concatenate([zpad, 0.5 * a_up[d]], axis=0).astype(BF16),
            row(k_k), row(k_a)))
    final_extra = (row(r_k), row(gn_gain), row(gn_bias))
    bound = (SCORE_BOUND_COEF * jnp.max(jnp.abs(q_norm_gain)) * jnp.max(jnp.abs(k_norm_gain))
             ).astype(F32).reshape(1, 1)
    attn = (row(jnp.tile(q_norm_gain, 2)), row(jnp.tile(k_norm_gain, 2)), row(subln_gain), bound)
    proj_w = (row(norm_gain), w_main, w_lora, row(mu_shift[:d3]), row(mu_shift[d3:shift_cols]))
    return proj_w, dirs, final_extra, attn, w_out.astype(BF16)


def _layer(x, weights, lambda_qk, rope):
    proj_w, dirs, final_extra, attn, w_o = weights
    b, t, _ = x.shape
    x2 = x.reshape(b * t, D_MODEL)
    proj2, lora2 = _in_proj(x2, t, *proj_w)
    proj = proj2.reshape(b, t, D_MAIN)
    lora = lora2.reshape(b, t, 2 * LORA)

    o_f = _rwkv_call(False, False, proj, lora, dirs[0], None)
    y_r = _rwkv_call(True, True, proj, lora, dirs[1], final_extra + (o_f,))

    tk = 512
    cos, sin = rope
    q_gain, k_gain, subln, bound = attn
    qt, kn = _attn_prep(proj, q_gain, k_gain, cos, sin, tk)
    y_d = lax.cond(bound[0, 0] <= ATTN_BOUND_MAX,
                   functools.partial(_diff_attn, True), functools.partial(_diff_attn, False),
                   qt, kn, proj, lambda_qk, subln, bound)

    out = _out_proj(y_r.reshape(b * t, D_RWKV), y_d.reshape(b * t, D_DIFF), w_o, x2)
    return out.reshape(b, t, D_MODEL)


def kernel(x_prompt, x_sample, norm_gain, w_in, mu_shift, w0, w_up, a0, a_up, k_k, k_a, r_k, gn_gain,
           gn_bias, q_norm_gain, k_norm_gain, lambda_qk, subln_gain, w_out):
    weights = _prepare_weights(norm_gain[0], w_in[0], mu_shift[0], w0[0], w_up[0], a0[0], a_up[0],
                               k_k[0], k_a[0], r_k[0], gn_gain[0], gn_bias[0], q_norm_gain[0],
                               k_norm_gain[0], subln_gain[0], w_out[0])
    lam = lambda_qk[0].astype(F32)
    rope = _rope_tables(max(x_prompt.shape[1], x_sample.shape[1]))
    return (_layer(x_prompt, weights, lam, rope), _layer(x_sample, weights, lam, rope))
```
